```python
import math, functools
import jax, jax.numpy as jnp
from jax import lax
import numpy as np

D_MODEL = 2048
BATCH = 2
SEQ = 8192
DEPTH = 1
DEC_BATCH = 32
DEC_SEQ = 4
PAST_LEN = 16384
PAGE_SIZE = 128

DA_HEAD_DIM = 128
DA_HEADS = D_MODEL // (2 * DA_HEAD_DIM)
DA_QK = 2 * DA_HEADS * DA_HEAD_DIM
DA_V = DA_HEADS * 2 * DA_HEAD_DIM
GDN_HEADS = D_MODEL // 256
GDN_DK = 128
GDN_DV = 256
GDN_QK = GDN_HEADS * GDN_DK
GDN_V = GDN_HEADS * GDN_DV
GDN_CONV_DIM = 2 * GDN_QK + GDN_V
CONV_WIDTH = 4
GDN_CHUNK = 64
MEM_LEN = 256
MEM_HEADS = 4
MEM_HEAD_DIM = 128
MEM_W = MEM_HEADS * MEM_HEAD_DIM
Q_BLOCK = 128
NORM_EPS = 1e-6
SUBLN_EPS = 1e-5

kernel_name = 'hybrid_diffattn_gdn_memory_step'


def split_sizes():
    return (DA_QK, DA_QK, DA_V, DA_V,
            GDN_QK, GDN_QK, GDN_V, GDN_V, GDN_HEADS, GDN_HEADS,
            MEM_W, MEM_W,
            D_MODEL, D_MODEL, D_MODEL)


def split_proj(p):
    cuts = [int(c) for c in np.cumsum(split_sizes())[:-1]]
    return jnp.split(p, cuts, axis=-1)


def rms_norm(x, g, eps=NORM_EPS):
    xf = x.astype(jnp.float32)
    y = xf * lax.rsqrt(jnp.mean(xf * xf, axis=-1, keepdims=True) + eps)
    return (y * g.astype(jnp.float32)).astype(x.dtype)


def l2_normalize(x, eps=1e-6):
    xf = x.astype(jnp.float32)
    return xf * lax.rsqrt(jnp.sum(xf * xf, axis=-1, keepdims=True) + eps)


def lambda_init(layer):
    return 0.8 - 0.6 * math.exp(-0.3 * layer)


def diff_lambda(lq1, lk1, lq2, lk2, layer):
    f = lambda a: a.astype(jnp.float32)
    return jnp.exp(jnp.sum(f(lq1) * f(lk1))) - jnp.exp(jnp.sum(f(lq2) * f(lk2))) + lambda_init(layer)


def diff_attn_prompt(q, k, v, lam, layer):
    b, s = q.shape[0], q.shape[1]
    nb = s // Q_BLOCK
    scale = DA_HEAD_DIM ** -0.5
    q_blocks = jnp.moveaxis(q.reshape(b, nb, Q_BLOCK, DA_HEADS, 2, DA_HEAD_DIM), 1, 0)
    k_pos = jnp.arange(s)

    def one_block(args):
        qb, i = args
        sc = jnp.einsum('bqhcd,bkhcd->bhcqk', qb, k).astype(jnp.float32) * scale
        q_pos = i * Q_BLOCK + jnp.arange(Q_BLOCK)
        sc = jnp.where(k_pos[None, :] <= q_pos[:, None], sc, -jnp.inf)
        p = jax.nn.softmax(sc, axis=-1)
        pd = p[:, :, 0] - lam * p[:, :, 1]
        return jnp.einsum('bhqk,bkhe->bqhe', pd.astype(v.dtype), v)

    o = lax.map(one_block, (q_blocks, jnp.arange(nb)))
    return jnp.moveaxis(o, 0, 1).reshape(b, s, DA_HEADS, 2 * DA_HEAD_DIM)


def diff_attn_sample(q, k, v, lam, layer, cache_k, cache_v, page_table):
    t = q.shape[1]
    scale = DA_HEAD_DIM ** -0.5
    sc = jnp.einsum('bthcd,bshcd->bhcts', q, k).astype(jnp.float32) * scale
    causal = jnp.tril(jnp.ones((t, t), dtype=bool))
    sc = jnp.where(causal, sc, -jnp.inf)
    m = jnp.max(sc, axis=-1)
    p = jnp.exp(sc - m[..., None])
    l = jnp.sum(p, axis=-1)
    acc = jnp.einsum('bhcts,bshe->bhcte', p, v.astype(jnp.float32))

    def page_step(carry, pages):
        m, l, acc = carry
        kp = cache_k[layer, pages]
        vp = cache_v[layer, pages]
        s_ = jnp.einsum('bthcd,bshcd->bhcts', q, kp).astype(jnp.float32) * scale
        m_new = jnp.maximum(m, jnp.max(s_, axis=-1))
        corr = jnp.exp(m - m_new)
        p_ = jnp.exp(s_ - m_new[..., None])
        l_new = l * corr + jnp.sum(p_, axis=-1)
        acc_new = acc * corr[..., None] + jnp.einsum('bhcts,bshe->bhcte', p_, vp.astype(jnp.float32))
        return (m_new, l_new, acc_new), None

    (m, l, acc), _ = lax.scan(page_step, (m, l, acc), page_table.T)
    o = acc / l[..., None]
    od = o[:, :, 0] - lam * o[:, :, 1]
    return jnp.transpose(od, (0, 2, 1, 3)).astype(v.dtype)


def short_conv(x_new, conv_state, w):
    xp = jnp.concatenate([conv_state.astype(x_new.dtype), x_new], axis=1)
    L = x_new.shape[1]
    y = xp[:, 0:L] * w[0]
    for j in range(1, CONV_WIDTH):
        y = y + xp[:, j:j + L] * w[j]
    return jax.nn.silu(y), xp[:, -(CONV_WIDTH - 1):]


def gated_delta_rule(q, k, v, g, beta, s0):
    b, L, H, dk = q.shape
    dv = v.shape[-1]
    c = math.gcd(L, GDN_CHUNK)
    n = L // c

    def chunks(t):
        t = t.reshape(b, n, c, H, *t.shape[3:])
        return jnp.moveaxis(jnp.moveaxis(t, 1, 0), 3, 2)

    qc, kc, vc, gc, bc = chunks(q), chunks(k), chunks(v), chunks(g), chunks(beta)
    gcum = jnp.cumsum(gc, axis=-1)
    incl = jnp.tril(jnp.ones((c, c), dtype=bool))
    strict = jnp.tril(jnp.ones((c, c), dtype=bool), -1)
    decay = jnp.exp(jnp.where(incl, gcum[..., :, None] - gcum[..., None, :], -jnp.inf))
    kk = jnp.einsum('nbhid,nbhjd->nbhij', kc, kc)
    lmat = jnp.where(strict, bc[..., :, None] * kk * decay, 0.0)
    a = lmat + jnp.eye(c, dtype=jnp.float32)
    rhs = jnp.concatenate([bc[..., None] * vc, (bc * jnp.exp(gcum))[..., None] * kc], axis=-1)
    sol = lax.linalg.triangular_solve(a, rhs, left_side=True, lower=True, unit_diagonal=True)
    uv, wk = sol[..., :dv], sol[..., dv:]
    aqk = jnp.einsum('nbhid,nbhjd->nbhij', qc, kc) * decay
    gam = jnp.exp(gcum)
    gam_last = gam[..., -1]
    kdec = kc * jnp.exp(gcum[..., -1:] - gcum)[..., None]

    def step(s, xs):
        qi, uvi, wki, aqki, gami, gli, kdi = xs
        u = uvi - jnp.einsum('bhcd,bhde->bhce', wki, s)
        o = gami[..., None] * jnp.einsum('bhcd,bhde->bhce', qi, s) + jnp.einsum('bhij,bhje->bhie', aqki, u)
        s = gli[..., None, None] * s + jnp.einsum('bhcd,bhce->bhde', kdi, u)
        return s, o

    s_final, o = lax.scan(step, s0, (qc, uv, wk, aqk, gam, gam_last, kdec))
    o = jnp.moveaxis(jnp.moveaxis(o, 2, 3), 0, 1).reshape(b, L, H, dv)
    return o, s_final


def gdn_branch(q_raw, k_raw, v_raw, b_raw, a_raw, z, conv_state, s0, conv_w, a_log, dt_bias, norm_g):
    b, L = q_raw.shape[0], q_raw.shape[1]
    qkv, conv_new = short_conv(jnp.concatenate([q_raw, k_raw, v_raw], axis=-1), conv_state, conv_w)
    q = qkv[..., :GDN_QK].reshape(b, L, GDN_HEADS, GDN_DK)
    k = qkv[..., GDN_QK:2 * GDN_QK].reshape(b, L, GDN_HEADS, GDN_DK)
    v = qkv[..., 2 * GDN_QK:].reshape(b, L, GDN_HEADS, GDN_DV).astype(jnp.float32)
    q = l2_normalize(q) * (GDN_DK ** -0.5)
    k = l2_normalize(k)
    beta = jax.nn.sigmoid(b_raw.astype(jnp.float32))
    g = -jnp.exp(a_log.astype(jnp.float32)) * jax.nn.softplus(a_raw.astype(jnp.float32) + dt_bias.astype(jnp.float32))
    o, s_new = gated_delta_rule(q, k, v, g, beta, s0.astype(jnp.float32))
    o = rms_norm(o, norm_g).reshape(b, L, GDN_V).astype(z.dtype)
    return o * jax.nn.silu(z), conv_new, s_new.astype(z.dtype)


def mem_kv(mem, norm_g, w_kv):
    b, m = mem.shape[0], mem.shape[1]
    kv = rms_norm(mem, norm_g) @ w_kv
    mk = kv[..., :MEM_W].reshape(b, m, MEM_HEADS, MEM_HEAD_DIM)
    mv = kv[..., MEM_W:].reshape(b, m, MEM_HEADS, MEM_HEAD_DIM)
    return mk, mv


def mem_attn(q, mk, mv):
    sc = jnp.einsum('blhd,bmhd->bhlm', q, mk).astype(jnp.float32) * (MEM_HEAD_DIM ** -0.5)
    p = jax.nn.softmax(sc, axis=-1)
    return jnp.einsum('bhlm,bmhd->blhd', p.astype(mv.dtype), mv)


def mixer_layer(x, layer, attn_fn, conv_state, delta_state, mk, mv, w):
    (norm_g, w_in, lq1, lk1, lq2, lk2, subln_g, conv_w, a_log, dt_bias, gdn_norm_g,
     w_down_a, w_down_b, w_down_m, w_out) = w
    b, L = x.shape[0], x.shape[1]
    h = rms_norm(x, norm_g)
    (qa, ka, va, za, qb, kb, vb, zb, b_raw, a_raw, qm, zm, ga, gb, gm) = split_proj(h @ w_in)
    qa = qa.reshape(b, L, DA_HEADS, 2, DA_HEAD_DIM)
    ka = ka.reshape(b, L, DA_HEADS, 2, DA_HEAD_DIM)
    va = va.reshape(b, L, DA_HEADS, 2 * DA_HEAD_DIM)
    lam = diff_lambda(lq1, lk1, lq2, lk2, layer)
    oa = attn_fn(qa, ka, va, lam, layer)
    oa = rms_norm(oa, subln_g, SUBLN_EPS) * (1.0 - lambda_init(layer))
    ya = oa.reshape(b, L, DA_V) * jax.nn.silu(za)
    yb, conv_new, s_new = gdn_branch(qb, kb, vb, b_raw, a_raw, zb, conv_state, delta_state,
                                     conv_w, a_log, dt_bias, gdn_norm_g)
    om = mem_attn(qm.reshape(b, L, MEM_HEADS, MEM_HEAD_DIM), mk, mv)
    ym = om.reshape(b, L, MEM_W) * jax.nn.silu(zm)
    merged = (jax.nn.sigmoid(ga) * (ya @ w_down_a) + jax.nn.sigmoid(gb) * (yb @ w_down_b)
              + jax.nn.sigmoid(gm) * (ym @ w_down_m))
    return x + merged @ w_out, ka, va, conv_new, s_new


def setup_inputs(seed: int = 0) -> dict:
    key = jax.random.key(seed)
    ks = jax.random.split(key, 40)
    f32 = jnp.float32
    nrm = lambda k, shape, scale=1.0: jax.random.normal(k, shape, f32) * scale
    n_pages = PAST_LEN // PAGE_SIZE
    n_used = DEC_BATCH * n_pages
    n_phys = n_used + max(1, n_used // 4)
    n_in = sum(split_sizes())
    page_table = jax.random.permutation(ks[0], n_phys)[:n_used].reshape(DEC_BATCH, n_pages).astype(jnp.int32)
    a_init = jax.random.uniform(ks[1], (DEPTH, GDN_HEADS), f32, 1.0, 16.0)
    dt = jnp.exp(jax.random.uniform(ks[2], (DEPTH, GDN_HEADS), f32, math.log(1e-3), math.log(1e-1)))
    return {
        'x_prompt': nrm(ks[3], (BATCH, SEQ, D_MODEL)),
        'x_sample': nrm(ks[4], (DEC_BATCH, DEC_SEQ, D_MODEL)),
        'cache_k': nrm(ks[5], (DEPTH, n_phys, PAGE_SIZE, DA_HEADS, 2, DA_HEAD_DIM)),
        'cache_v': nrm(ks[6], (DEPTH, n_phys, PAGE_SIZE, DA_HEADS, 2 * DA_HEAD_DIM)),
        'cache_mem_k': nrm(ks[7], (DEPTH, DEC_BATCH, MEM_LEN, MEM_HEADS, MEM_HEAD_DIM)),
        'cache_mem_v': nrm(ks[8], (DEPTH, DEC_BATCH, MEM_LEN, MEM_HEADS, MEM_HEAD_DIM)),
        'state_delta': nrm(ks[9], (DEPTH, DEC_BATCH, GDN_HEADS, GDN_DK, GDN_DV), GDN_DK ** -0.5),
        'state_conv': nrm(ks[10], (DEPTH, DEC_BATCH, CONV_WIDTH - 1, GDN_CONV_DIM)),
        'page_table': page_table,
        'mem_prompt': nrm(ks[11], (BATCH, MEM_LEN, D_MODEL)),
        'norm_g': 1.0 + nrm(ks[12], (DEPTH, D_MODEL), 0.02),
        'w_in': nrm(ks[13], (DEPTH, D_MODEL, n_in), D_MODEL ** -0.5),
        'lambda_q1': nrm(ks[14], (DEPTH, DA_HEAD_DIM), 0.1),
        'lambda_k1': nrm(ks[15], (DEPTH, DA_HEAD_DIM), 0.1),
        'lambda_q2': nrm(ks[16], (DEPTH, DA_HEAD_DIM), 0.1),
        'lambda_k2': nrm(ks[17], (DEPTH, DA_HEAD_DIM), 0.1),
        'subln_g': 1.0 + nrm(ks[18], (DEPTH, 2 * DA_HEAD_DIM), 0.02),
        'conv_w': nrm(ks[19], (DEPTH, CONV_WIDTH, GDN_CONV_DIM), CONV_WIDTH ** -0.5),
        'a_log': jnp.log(a_init),
        'dt_bias': dt + jnp.log(-jnp.expm1(-dt)),
        'gdn_norm_g': 1.0 + nrm(ks[20], (DEPTH, GDN_DV), 0.02),
        'w_down_a': nrm(ks[21], (DEPTH, DA_V, D_MODEL), DA_V ** -0.5),
        'w_down_b': nrm(ks[22], (DEPTH, GDN_V, D_MODEL), GDN_V ** -0.5),
        'w_down_m': nrm(ks[23], (DEPTH, MEM_W, D_MODEL), MEM_W ** -0.5),
        'w_out': nrm(ks[24], (DEPTH, D_MODEL, D_MODEL), D_MODEL ** -0.5),
        'mem_norm_g': 1.0 + nrm(ks[25], (DEPTH, D_MODEL), 0.02),
        'w_mem_kv': nrm(ks[26], (DEPTH, D_MODEL, 2 * MEM_W), D_MODEL ** -0.5),
        'final_norm_g': 1.0 + nrm(ks[27], (D_MODEL,), 0.02),
    }


def reference(x_prompt, x_sample, cache_k, cache_v, cache_mem_k, cache_mem_v, state_delta, state_conv,
              page_table, mem_prompt, norm_g, w_in, lambda_q1, lambda_k1, lambda_q2, lambda_k2, subln_g,
              conv_w, a_log, dt_bias, gdn_norm_g, w_down_a, w_down_b, w_down_m, w_out, mem_norm_g,
              w_mem_kv, final_norm_g):
    b = x_prompt.shape[0]
    sample_attn = functools.partial(diff_attn_sample, cache_k=cache_k, cache_v=cache_v, page_table=page_table)
    conv0 = jnp.zeros((b, CONV_WIDTH - 1, GDN_CONV_DIM), x_prompt.dtype)
    s0 = jnp.zeros((b, GDN_HEADS, GDN_DK, GDN_DV), jnp.float32)
    hp, hs = x_prompt, x_sample
    k_p, v_p, k_s, v_s, mk_p, mv_p, d_p, d_s, c_p, c_s = ([] for _ in range(10))
    for l in range(DEPTH):
        w = (norm_g[l], w_in[l], lambda_q1[l], lambda_k1[l], lambda_q2[l], lambda_k2[l], subln_g[l],
             conv_w[l], a_log[l], dt_bias[l], gdn_norm_g[l], w_down_a[l], w_down_b[l], w_down_m[l], w_out[l])
        mk, mv = mem_kv(mem_prompt, mem_norm_g[l], w_mem_kv[l])
        hp, kp_, vp_, cp_, sp_ = mixer_layer(hp, l, diff_attn_prompt, conv0, s0, mk, mv, w)
        hs, ks_, vs_, cs_, ss_ = mixer_layer(hs, l, sample_attn, state_conv[l], state_delta[l],
                                             cache_mem_k[l], cache_mem_v[l], w)
        k_p.append(kp_); v_p.append(vp_); k_s.append(ks_); v_s.append(vs_)
        mk_p.append(mk); mv_p.append(mv)
        d_p.append(sp_); d_s.append(ss_); c_p.append(cp_); c_s.append(cs_)
    y_prompt = rms_norm(hp, final_norm_g)
    y_sample = rms_norm(hs, final_norm_g)
    return (y_prompt, y_sample, jnp.stack(k_p), jnp.stack(v_p), jnp.stack(k_s), jnp.stack(v_s),
            jnp.stack(mk_p), jnp.stack(mv_p), jnp.stack(d_p), jnp.stack(d_s), jnp.stack(c_p), jnp.stack(c_s))
```

```python
import functools
import math

import jax
import jax.numpy as jnp
import numpy as np
from jax import lax
from jax.experimental import pallas as pl
from jax.experimental.pallas import tpu as pltpu

F32 = jnp.float32
BF16 = jnp.bfloat16

D_MODEL = 2048
PAGE_SIZE = 128
DA_HEAD_DIM = 128
DA_HEADS = 8
DA_HEAD_W = 2 * DA_HEAD_DIM
GDN_HEADS = 8
GDN_DK = 128
GDN_DV = 256
GDN_QK = GDN_HEADS * GDN_DK
GDN_V = GDN_HEADS * GDN_DV
GDN_CONV_DIM = 2 * GDN_QK + GDN_V
CONV_WIDTH = 4
GDN_CHUNK = 64
GDN_GROUP = 4
MEM_LEN = 256
MEM_HEADS = 4
MEM_HEAD_DIM = 128
MEM_W = MEM_HEADS * MEM_HEAD_DIM
NORM_EPS = 1e-6
SUBLN_EPS = 1e-5
LAMBDA_INIT = 0.8 - 0.6 * math.exp(-0.3 * 0)
SAMPLE_ROWS = 8

R_QKV = 0
R_ZA = 4096
R_ZB = 6144
R_GA = 8192
R_GB = 10240
R_GM = 12288
R_QM = 14336
R_ZM = 14848
R_WIDTH = 15360

VMEM_LIMIT = 56 * 1024 * 1024


def _cparams(sem):
    return pltpu.CompilerParams(dimension_semantics=sem, vmem_limit_bytes=VMEM_LIMIT)


def _silu(x):
    return x * jax.nn.sigmoid(x)


def _dot(a, b):
    return jnp.dot(a.astype(BF16), b.astype(BF16), preferred_element_type=F32)


def _dot_nt(a, b):
    return lax.dot_general(a.astype(BF16), b.astype(BF16), (((1,), (1,)), ((), ())),
                           preferred_element_type=F32)


def _rmsnorm_kernel(x_ref, g_ref, o_ref, *, eps):
    x = x_ref[...].astype(F32)
    y = x * lax.rsqrt(jnp.mean(x * x, axis=-1, keepdims=True) + eps)
    o_ref[...] = (y * g_ref[...]).astype(o_ref.dtype)


def _rmsnorm(x2d, g, out_dtype, tm):
    m, d = x2d.shape
    tm = min(tm, m)
    return pl.pallas_call(
        functools.partial(_rmsnorm_kernel, eps=NORM_EPS),
        grid=(m // tm,),
        in_specs=[pl.BlockSpec((tm, d), lambda i: (i, 0)),
                  pl.BlockSpec((1, d), lambda i: (0, 0))],
        out_specs=pl.BlockSpec((tm, d), lambda i: (i, 0)),
        out_shape=jax.ShapeDtypeStruct((m, d), out_dtype),
        compiler_params=_cparams(("parallel",)),
        name="rmsnorm",
    )(x2d, g.reshape(1, d).astype(F32))


def _matmul_kernel(h_ref, w_ref, *o_refs, scale):
    acc = jnp.dot(h_ref[...], w_ref[...], preferred_element_type=F32)
    if scale is not None:
        acc = acc * scale
    for o_ref in o_refs:
        o_ref[...] = acc.astype(o_ref.dtype)


def _matmul(h, w, out_dtypes, tm, tn, scale=None, name="matmul"):
    m, k = h.shape
    n = w.shape[1]
    tm, tn = min(tm, m), min(tn, n)
    outs = pl.pallas_call(
        functools.partial(_matmul_kernel, scale=scale),
        grid=(m // tm, n // tn),
        in_specs=[pl.BlockSpec((tm, k), lambda i, j: (i, 0)),
                  pl.BlockSpec((k, tn), lambda i, j: (0, j))],
        out_specs=[pl.BlockSpec((tm, tn), lambda i, j: (i, j)) for _ in out_dtypes],
        out_shape=[jax.ShapeDtypeStruct((m, n), dt) for dt in out_dtypes],
        compiler_params=_cparams(("parallel", "parallel")),
        name=name,
    )(h, w)
    return outs


def _diff_lambda(lq1_ref, lk1_ref, lq2_ref, lk2_ref):
    s1 = jnp.sum(lq1_ref[...] * lk1_ref[...], axis=-1, keepdims=True)
    s2 = jnp.sum(lq2_ref[...] * lk2_ref[...], axis=-1, keepdims=True)
    return jnp.exp(s1) - jnp.exp(s2) + LAMBDA_INIT


def _subln_gate(od, g, z):
    y = od * lax.rsqrt(jnp.mean(od * od, axis=-1, keepdims=True) + SUBLN_EPS)
    y = (y * g) * (1.0 - LAMBDA_INIT)
    return y * _silu(z)


def _attn_prompt_kernel(qi_tab, ki_tab, q_ref, k_ref, v_ref, z_ref, g_ref,
                        lq1_ref, lk1_ref, lq2_ref, lk2_ref, o_ref, m_sc, l_sc, acc_sc):
    step = pl.program_id(2)
    qi = qi_tab[step]
    ki = ki_tab[step]
    tq = q_ref.shape[0]

    @pl.when(ki == 0)
    def _():
        m_sc[...] = jnp.full(m_sc.shape, -jnp.inf, F32)
        l_sc[...] = jnp.zeros(l_sc.shape, F32)
        acc_sc[...] = jnp.zeros(acc_sc.shape, F32)

    def update(masked):
        q = q_ref[...]
        k = k_ref[...]
        v = v_ref[...]
        if masked:
            row = lax.broadcasted_iota(jnp.int32, (tq, tq), 0)
            col = lax.broadcasted_iota(jnp.int32, (tq, tq), 1)
            keep = col <= row
        for c in range(2):
            lo, hi = c * DA_HEAD_DIM, (c + 1) * DA_HEAD_DIM
            s = lax.dot_general(q[:, lo:hi], k[:, lo:hi], (((1,), (1,)), ((), ())),
                                preferred_element_type=F32)
            if masked:
                s = jnp.where(keep, s, -jnp.inf)
            m_prev = m_sc[c]
            m_new = jnp.maximum(m_prev, jnp.max(s, axis=-1, keepdims=True))
            alpha = jnp.exp(m_prev - m_new)
            p = jnp.exp(s - m_new)
            l_sc[c] = alpha * l_sc[c] + jnp.sum(p, axis=-1, keepdims=True)
            acc_sc[c] = alpha * acc_sc[c] + jnp.dot(p.astype(BF16), v, preferred_element_type=F32)
            m_sc[c] = m_new

    @pl.when(ki < qi)
    def _():
        update(False)

    @pl.when(ki == qi)
    def _():
        update(True)
        lam = _diff_lambda(lq1_ref, lk1_ref, lq2_ref, lk2_ref)
        od = acc_sc[0] / l_sc[0] - lam * (acc_sc[1] / l_sc[1])
        o_ref[...] = _subln_gate(od, g_ref[...], z_ref[...].astype(F32)).astype(o_ref.dtype)


def _attn_prompt(q, kv, rest, subln_g, lam_vecs, batch, seq, tq):
    m = batch * seq
    nq = seq // tq
    qi_tab = np.concatenate([np.full(i + 1, i, np.int32) for i in range(nq)])
    ki_tab = np.concatenate([np.arange(i + 1, dtype=np.int32) for i in range(nq)])
    n_steps = int(qi_tab.shape[0])
    za_blk = R_ZA // DA_HEAD_W
    vec = pl.BlockSpec((1, DA_HEAD_DIM), lambda b, h, s, qt, kt: (0, 0))
    grid_spec = pltpu.PrefetchScalarGridSpec(
        num_scalar_prefetch=2,
        grid=(batch, DA_HEADS, n_steps),
        in_specs=[
            pl.BlockSpec((tq, DA_HEAD_W), lambda b, h, s, qt, kt: (b * nq + qt[s], h)),
            pl.BlockSpec((tq, DA_HEAD_W), lambda b, h, s, qt, kt: (b * nq + kt[s], h)),
            pl.BlockSpec((tq, DA_HEAD_W), lambda b, h, s, qt, kt: (b * nq + kt[s], DA_HEADS + h)),
            pl.BlockSpec((tq, DA_HEAD_W), lambda b, h, s, qt, kt: (b * nq + qt[s], za_blk + h)),
            pl.BlockSpec((1, DA_HEAD_W), lambda b, h, s, qt, kt: (0, 0)),
            vec, vec, vec, vec,
        ],
        out_specs=pl.BlockSpec((tq, DA_HEAD_W), lambda b, h, s, qt, kt: (b * nq + qt[s], h)),
        scratch_shapes=[pltpu.VMEM((2, tq, 1), F32), pltpu.VMEM((2, tq, 1), F32),
                        pltpu.VMEM((2, tq, DA_HEAD_W), F32)],
    )
    return pl.pallas_call(
        _attn_prompt_kernel,
        grid_spec=grid_spec,
        out_shape=jax.ShapeDtypeStruct((m, DA_HEADS * DA_HEAD_W), BF16),
        compiler_params=_cparams(("parallel", "parallel", "arbitrary")),
        name="attn_prompt",
    )(jnp.asarray(qi_tab), jnp.asarray(ki_tab), q, kv, kv, rest, subln_g, *lam_vecs)


def _attn_paged_kernel(pt_ref, qb_ref, kn_ref, vn_ref, z_ref, g_ref,
                       lq1_ref, lk1_ref, lq2_ref, lk2_ref, *refs, pages_per_step, n_new):
    k_refs = refs[:pages_per_step]
    v_refs = refs[pages_per_step:2 * pages_per_step]
    o_ref = refs[2 * pages_per_step]
    m_sc, l_sc, acc_sc = refs[2 * pages_per_step + 1:]
    j = pl.program_id(1)
    rows = qb_ref.shape[2]

    def update(h, k_t, v_t, keep):
        s = lax.dot_general(qb_ref[0, h], k_t, (((1,), (1,)), ((), ())), preferred_element_type=F32)
        if keep is not None:
            s = jnp.where(keep, s, -jnp.inf)
        m_prev = m_sc[h]
        m_new = jnp.maximum(m_prev, jnp.max(s, axis=-1, keepdims=True))
        alpha = jnp.exp(m_prev - m_new)
        p = jnp.exp(s - m_new)
        l_sc[h] = alpha * l_sc[h] + jnp.sum(p, axis=-1, keepdims=True)
        acc_sc[h] = alpha * acc_sc[h] + jnp.dot(p.astype(BF16), v_t, preferred_element_type=F32)
        m_sc[h] = m_new

    @pl.when(j == 0)
    def _():
        m_sc[...] = jnp.full(m_sc.shape, -jnp.inf, F32)
        l_sc[...] = jnp.zeros(l_sc.shape, F32)
        acc_sc[...] = jnp.zeros(acc_sc.shape, F32)
        row = lax.broadcasted_iota(jnp.int32, (rows, PAGE_SIZE), 0)
        col = lax.broadcasted_iota(jnp.int32, (rows, PAGE_SIZE), 1)
        tok = jnp.where(row >= n_new, row - n_new, row)
        keep = col <= tok
        kn = kn_ref[0]
        vn = vn_ref[0]
        pad = jnp.zeros((PAGE_SIZE - kn.shape[0], DA_HEAD_W), F32)
        for h in range(DA_HEADS):
            lo, hi = h * DA_HEAD_W, (h + 1) * DA_HEAD_W
            k_t = jnp.concatenate([kn[:, lo:hi], pad], axis=0).astype(BF16)
            v_t = jnp.concatenate([vn[:, lo:hi], pad], axis=0).astype(BF16)
            update(h, k_t, v_t, keep)

    for r in range(pages_per_step):
        for h in range(DA_HEADS):
            lo, hi = h * DA_HEAD_W, (h + 1) * DA_HEAD_W
            update(h, k_refs[r][0, :, lo:hi].astype(BF16), v_refs[r][0, :, lo:hi].astype(BF16), None)

    @pl.when(j == pl.num_programs(1) - 1)
    def _():
        lam = _diff_lambda(lq1_ref, lk1_ref, lq2_ref, lk2_ref)
        g = g_ref[...]
        live = lax.broadcasted_iota(jnp.int32, (rows, 1), 0) < n_new
        for h in range(DA_HEADS):
            lo, hi = h * DA_HEAD_W, (h + 1) * DA_HEAD_W
            o = acc_sc[h] / l_sc[h]
            od = jnp.where(live, o - lam * pltpu.roll(o, shift=n_new, axis=0), 0.0)
            z = z_ref[0, :, lo:hi].astype(F32)
            o_ref[0, :, lo:hi] = _subln_gate(od, g, z).astype(o_ref.dtype)


def _attn_paged(qblk, k_new, v_new, z3, subln_g, lam_vecs, cache_k, cache_v, page_table,
                n_new, pages_per_step):
    bd, n_pages = page_table.shape
    n_steps = n_pages // pages_per_step
    width = DA_HEADS * DA_HEAD_W
    pt_flat = page_table.reshape(-1)

    def page_spec(r):
        return pl.BlockSpec((1, PAGE_SIZE, width),
                            lambda b, j, pt: (pt[b * n_pages + j * pages_per_step + r], 0, 0))

    vec = pl.BlockSpec((1, DA_HEAD_DIM), lambda b, j, pt: (0, 0))
    rows = k_new.shape[1]
    grid_spec = pltpu.PrefetchScalarGridSpec(
        num_scalar_prefetch=1,
        grid=(bd, n_steps),
        in_specs=[
            pl.BlockSpec((1, DA_HEADS, 2 * n_new, DA_HEAD_W), lambda b, j, pt: (b, 0, 0, 0)),
            pl.BlockSpec((1, rows, width), lambda b, j, pt: (b, 0, 0)),
            pl.BlockSpec((1, rows, width), lambda b, j, pt: (b, 0, 0)),
            pl.BlockSpec((1, rows, width), lambda b, j, pt: (b, 0, R_ZA // width)),
            pl.BlockSpec((1, DA_HEAD_W), lambda b, j, pt: (0, 0)),
            vec, vec, vec, vec,
        ] + [page_spec(r) for r in range(pages_per_step)] * 2,
        out_specs=pl.BlockSpec((1, rows, width), lambda b, j, pt: (b, 0, 0)),
        scratch_shapes=[pltpu.VMEM((DA_HEADS, 2 * n_new, 1), F32),
                        pltpu.VMEM((DA_HEADS, 2 * n_new, 1), F32),
                        pltpu.VMEM((DA_HEADS, 2 * n_new, DA_HEAD_W), F32)],
    )
    return pl.pallas_call(
        functools.partial(_attn_paged_kernel, pages_per_step=pages_per_step, n_new=n_new),
        grid_spec=grid_spec,
        out_shape=jax.ShapeDtypeStruct((bd, rows, width), F32),
        compiler_params=_cparams(("parallel", "arbitrary")),
        name="attn_paged",
    )(pt_flat, qblk, k_new, v_new, z3, subln_g, *lam_vecs,
      *([cache_k] * pages_per_step), *([cache_v] * pages_per_step))


def _mem_attn_kernel(q_ref, z_ref, mk_ref, mv_ref, o_ref):
    scale = MEM_HEAD_DIM ** -0.5
    for h in range(MEM_HEADS):
        lo, hi = h * MEM_HEAD_DIM, (h + 1) * MEM_HEAD_DIM
        s = _dot_nt(q_ref[0, :, lo:hi], mk_ref[0, :, lo:hi]) * scale
        s = s - jnp.max(s, axis=-1, keepdims=True)
        e = jnp.exp(s)
        p = e / jnp.sum(e, axis=-1, keepdims=True)
        o = _dot(p, mv_ref[0, :, lo:hi])
        o_ref[0, :, lo:hi] = (o * _silu(z_ref[0, :, lo:hi].astype(F32))).astype(o_ref.dtype)


def _mem_attn(rest3, mk, mv, tm):
    b, length, _ = rest3.shape
    tm = min(tm, length)
    return pl.pallas_call(
        _mem_attn_kernel,
        grid=(b, length // tm),
        in_specs=[pl.BlockSpec((1, tm, MEM_W), lambda i, j: (i, j, R_QM // MEM_W)),
                  pl.BlockSpec((1, tm, MEM_W), lambda i, j: (i, j, R_ZM // MEM_W)),
                  pl.BlockSpec((1, MEM_LEN, MEM_W), lambda i, j: (i, 0, 0)),
                  pl.BlockSpec((1, MEM_LEN, MEM_W), lambda i, j: (i, 0, 0))],
        out_specs=pl.BlockSpec((1, tm, MEM_W), lambda i, j: (i, j, 0)),
        out_shape=jax.ShapeDtypeStruct((b, length, MEM_W), BF16),
        compiler_params=_cparams(("parallel", "parallel")),
        name="mem_attn",
    )(rest3, rest3, mk, mv)


def _softplus(x):
    return jnp.maximum(x, 0.0) + jnp.log1p(jnp.exp(-jnp.abs(x)))


def _l2n(x):
    return x * lax.rsqrt(jnp.sum(x * x, axis=-1, keepdims=True) + 1e-6)


def _gdn_kernel(qkv_ref, zb_ref, ba_ref, conv0_ref, s0_ref, convw_ref, alog_ref, dtb_ref, gng_ref,
                yb_ref, sout_ref, convout_ref, xp_sc, s_sc, *, valid_len):
    i = pl.program_id(1)
    c = GDN_CHUNK
    gw = GDN_GROUP * c
    tail = CONV_WIDTH - 1

    @pl.when(i == 0)
    def _():
        xp_sc[8 - tail:8, :] = conv0_ref[0]
        s_sc[...] = s0_ref[0]

    xp_sc[8:8 + c, :] = qkv_ref[0].astype(F32)
    w = convw_ref[...]
    y = xp_sc[8 - tail:8 - tail + c, :] * w[0:1]
    for t in range(1, CONV_WIDTH):
        y = y + xp_sc[8 - tail + t:8 - tail + t + c, :] * w[t:t + 1]
    qkv = _silu(y)

    @pl.when(i == pl.num_programs(1) - 1)
    def _():
        convout_ref[0] = xp_sc[8 + valid_len - tail:8 + valid_len, :]

    xp_sc[8 - tail:8, :] = xp_sc[8 + c - tail:8 + c, :]

    ba = ba_ref[0]
    if valid_len < c:
        live = lax.broadcasted_iota(jnp.int32, (c, 1), 0) < valid_len

    ii = lax.broadcasted_iota(jnp.int32, (gw, gw), 0)
    jj = lax.broadcasted_iota(jnp.int32, (gw, gw), 1)
    shift = int(math.log2(c))
    same = lax.shift_right_logical(ii, shift) == lax.shift_right_logical(jj, shift)
    incl = same & (jj <= ii)
    strict = same & (jj < ii)
    upper = same & (ii <= jj)
    eye = ii == jj

    for grp in range(GDN_HEADS // GDN_GROUP):
        heads = [grp * GDN_GROUP + t for t in range(GDN_GROUP)]
        qs, ks, vs, bcols, gcols = [], [], [], [], []
        for h in heads:
            qs.append(_l2n(qkv[:, h * GDN_DK:(h + 1) * GDN_DK]) * (GDN_DK ** -0.5))
            ks.append(_l2n(qkv[:, GDN_QK + h * GDN_DK:GDN_QK + (h + 1) * GDN_DK]))
            vs.append(qkv[:, 2 * GDN_QK + h * GDN_DV:2 * GDN_QK + (h + 1) * GDN_DV])
            beta = jax.nn.sigmoid(ba[:, h:h + 1])
            g = -jnp.exp(alog_ref[0:1, h:h + 1]) * _softplus(
                ba[:, GDN_HEADS + h:GDN_HEADS + h + 1] + dtb_ref[0:1, h:h + 1])
            if valid_len < c:
                beta = jnp.where(live, beta, 0.0)
                g = jnp.where(live, g, 0.0)
            bcols.append(beta)
            gcols.append(g)
        q_st = jnp.concatenate(qs, axis=0)
        k_st = jnp.concatenate(ks, axis=0)
        v_st = jnp.concatenate(vs, axis=0)
        bcol = jnp.concatenate(bcols, axis=0)
        gcol = jnp.concatenate(gcols, axis=0)

        grow = jnp.sum(jnp.where(eye, gcol, 0.0), axis=0, keepdims=True)
        gc_col = jnp.sum(jnp.where(incl, grow, 0.0), axis=1, keepdims=True)
        gc_row = jnp.sum(jnp.where(upper, gcol, 0.0), axis=0, keepdims=True)
        dmat = jnp.exp(jnp.where(incl, gc_col - gc_row, -jnp.inf))
        egc = jnp.exp(gc_col)

        kk = _dot_nt(k_st, k_st)
        pw = -jnp.where(strict, bcol * kk * dmat, 0.0)
        yinv = pw
        for _ in range(int(math.log2(c)) - 1):
            pw = _dot(pw, pw)
            yinv = yinv + pw + _dot(yinv, pw)

        rhs_v = bcol * v_st
        rhs_k = (bcol * egc) * k_st
        uv = rhs_v + _dot(yinv, rhs_v)
        wk = rhs_k + _dot(yinv, rhs_k)
        aqk = _dot_nt(q_st, k_st) * dmat

        us, qss, s_olds = [], [], []
        for t, h in enumerate(heads):
            r0, r1 = t * c, (t + 1) * c
            s_old = s_sc[h]
            res = _dot(jnp.concatenate([wk[r0:r1], q_st[r0:r1]], axis=0), s_old)
            us.append(uv[r0:r1] - res[0:c])
            qss.append(res[c:2 * c])
            s_olds.append(s_old)
        au = _dot(aqk, jnp.concatenate(us, axis=0))
        for t, h in enumerate(heads):
            r0, r1 = t * c, (t + 1) * c
            o = egc[r0:r1] * qss[t] + au[r0:r1]
            gc_last = gc_col[r1 - 1:r1]
            kdec = k_st[r0:r1] * jnp.exp(gc_last - gc_col[r0:r1])
            s_sc[h] = jnp.exp(gc_last) * s_olds[t] + _dot(kdec.T, us[t])
            on = o * lax.rsqrt(jnp.mean(o * o, axis=-1, keepdims=True) + NORM_EPS) * gng_ref[...]
            z = zb_ref[0, :, h * GDN_DV:(h + 1) * GDN_DV].astype(F32)
            yb_ref[0, :, h * GDN_DV:(h + 1) * GDN_DV] = (on * _silu(z)).astype(yb_ref.dtype)

    @pl.when(i == pl.num_programs(1) - 1)
    def _():
        sout_ref[0] = s_sc[...]


def _gdn(qkv3, qkv_blk, zb3, zb_blk, ba3, conv0, s0, conv_w, a_log, dt_bias, gng, valid_len):
    b, length = ba3.shape[0], ba3.shape[1]
    c = GDN_CHUNK
    n = length // c
    tail = CONV_WIDTH - 1
    const = lambda *shape: pl.BlockSpec(shape, lambda bi, i: (0,) * len(shape))
    return pl.pallas_call(
        functools.partial(_gdn_kernel, valid_len=valid_len),
        grid=(b, n),
        in_specs=[
            pl.BlockSpec((1, c, GDN_CONV_DIM), lambda bi, i: (bi, i, qkv_blk)),
            pl.BlockSpec((1, c, GDN_V), lambda bi, i: (bi, i, zb_blk)),
            pl.BlockSpec((1, c, 128), lambda bi, i: (bi, i, 0)),
            pl.BlockSpec((1, tail, GDN_CONV_DIM), lambda bi, i: (bi, 0, 0)),
            pl.BlockSpec((1, GDN_HEADS, GDN_DK, GDN_DV), lambda bi, i: (bi, 0, 0, 0)),
            const(CONV_WIDTH, GDN_CONV_DIM), const(1, GDN_HEADS), const(1, GDN_HEADS), const(1, GDN_DV),
        ],
        out_specs=[
            pl.BlockSpec((1, c, GDN_V), lambda bi, i: (bi, i, 0)),
            pl.BlockSpec((1, GDN_HEADS, GDN_DK, GDN_DV), lambda bi, i: (bi, 0, 0, 0)),
            pl.BlockSpec((1, tail, GDN_CONV_DIM), lambda bi, i: (bi, 0, 0)),
        ],
        out_shape=[
            jax.ShapeDtypeStruct((b, length, GDN_V), BF16),
            jax.ShapeDtypeStruct((b, GDN_HEADS, GDN_DK, GDN_DV), F32),
            jax.ShapeDtypeStruct((b, tail, GDN_CONV_DIM), F32),
        ],
        scratch_shapes=[pltpu.VMEM((8 + c, GDN_CONV_DIM), F32),
                        pltpu.VMEM((GDN_HEADS, GDN_DK, GDN_DV), F32)],
        compiler_params=_cparams(("parallel", "arbitrary")),
        name="gdn",
    )(qkv3, zb3, ba3, conv0, s0, conv_w.astype(F32), a_log.reshape(1, -1).astype(F32),
      dt_bias.reshape(1, -1).astype(F32), gng.reshape(1, -1).astype(F32))


def _out_kernel(x_ref, ya_ref, yb_ref, ym_ref, ga_ref, gb_ref, gm_ref,
                wda_ref, wdb_ref, wdm_ref, wout_ref, fg_ref, o_ref):
    sig = lambda r: jax.nn.sigmoid(r[...].astype(F32))
    merged = sig(ga_ref) * _dot(ya_ref[...], wda_ref[...])
    merged = merged + sig(gb_ref) * _dot(yb_ref[...], wdb_ref[...])
    merged = merged + sig(gm_ref) * _dot(ym_ref[...], wdm_ref[...])
    r = x_ref[...] + _dot(merged, wout_ref[...])
    y = r * lax.rsqrt(jnp.mean(r * r, axis=-1, keepdims=True) + NORM_EPS)
    o_ref[...] = y * fg_ref[...]


def _out_proj(x2d, ya, yb, ym, rest, wda, wdb, wdm, wout, final_g, tm):
    m, d = x2d.shape
    tm = min(tm, m)
    row = lambda width, blk: pl.BlockSpec((tm, width), lambda i: (i, blk))
    resident = lambda shape: pl.BlockSpec(shape, lambda i: (0, 0), pipeline_mode=pl.Buffered(1))
    return pl.pallas_call(
        _out_kernel,
        grid=(m // tm,),
        in_specs=[row(d, 0), row(d, 0), row(d, 0), row(MEM_W, 0),
                  row(d, R_GA // d), row(d, R_GB // d), row(d, R_GM // d),
                  resident(wda.shape), resident(wdb.shape), resident(wdm.shape), resident(wout.shape),
                  resident((1, d))],
        out_specs=row(d, 0),
        out_shape=jax.ShapeDtypeStruct((m, d), F32),
        compiler_params=_cparams(("parallel",)),
        name="out_proj",
    )(x2d, ya, yb, ym, rest, rest, rest, wda, wdb, wdm, wout, final_g.reshape(1, d).astype(F32))


def _project(h, w_q, w_kv, w_rest, w_ba, tm):
    scale = DA_HEAD_DIM ** -0.5
    (q,) = _matmul(h, w_q, [BF16], tm, 1024, scale=scale, name="proj_q")
    kv_f32, kv = _matmul(h, w_kv, [F32, BF16], tm, 1024, name="proj_kv")
    (rest,) = _matmul(h, w_rest, [BF16], tm, 1024, name="proj_rest")
    (ba,) = _matmul(h, w_ba, [F32], tm, 128, name="proj_ba")
    return q, kv_f32, kv, rest, ba


def kernel(x_prompt, x_sample, cache_k, cache_v, cache_mem_k, cache_mem_v, state_delta, state_conv, page_table, mem_prompt, norm_g, w_in, lambda_q1, lambda_k1, lambda_q2, lambda_k2, subln_g, conv_w, a_log, dt_bias, gdn_norm_g, w_down_a, w_down_b, w_down_m, w_out, mem_norm_g, w_mem_kv, final_norm_g):
    assert w_in.shape[0] == 1, "single-layer trunk"
    batch, seq, d = x_prompt.shape
    bd, n_new, _ = x_sample.shape
    assert d == D_MODEL and 2 * n_new == SAMPLE_ROWS and n_new >= CONV_WIDTH - 1
    assert seq % 512 == 0 and seq % GDN_CHUNK == 0

    w = w_in[0]
    o_qa, o_ka, o_za, o_qb, o_zb, o_b, o_qm, o_ga = 0, 2048, 6144, 8192, 12288, 14336, 14352, 15376
    w_q = w[:, o_qa:o_ka].astype(BF16)
    w_kv = w[:, o_ka:o_za].astype(BF16)
    w_rest = jnp.concatenate([w[:, o_qb:o_zb], w[:, o_za:o_qb], w[:, o_zb:o_b], w[:, o_ga:], w[:, o_qm:o_ga]],
                             axis=1).astype(BF16)
    w_ba = jnp.pad(w[:, o_b:o_qm], ((0, 0), (0, 128 - 2 * GDN_HEADS))).astype(BF16)
    wda, wdb, wdm, wout = (t[0].astype(BF16) for t in (w_down_a, w_down_b, w_down_m, w_out))
    lam_vecs = [t[0].reshape(1, DA_HEAD_DIM).astype(F32) for t in (lambda_q1, lambda_k1, lambda_q2, lambda_k2)]
    sg = subln_g[0].reshape(1, DA_HEAD_W).astype(F32)

    xp2 = x_prompt.reshape(batch * seq, d)
    hp = _rmsnorm(xp2, norm_g[0], BF16, 512)
    q_p, kv_p_f32, kv_p, rest_p, ba_p = _project(hp, w_q, w_kv, w_rest, w_ba, 1024)
    ya_p = _attn_prompt(q_p, kv_p, rest_p, sg, lam_vecs, batch, seq, 512)

    hm = _rmsnorm(mem_prompt.reshape(batch * MEM_LEN, d), mem_norm_g[0], BF16, 512)
    (mem_kv,) = _matmul(hm, w_mem_kv[0].astype(BF16), [F32], 512, 1024, name="proj_mem_kv")
    mk_p = mem_kv[:, :MEM_W].reshape(batch, MEM_LEN, MEM_W)
    mv_p = mem_kv[:, MEM_W:].reshape(batch, MEM_LEN, MEM_W)
    rest_p3 = rest_p.reshape(batch, seq, R_WIDTH)
    ym_p = _mem_attn(rest_p3, mk_p, mv_p, 512)

    conv0 = jnp.zeros((batch, CONV_WIDTH - 1, GDN_CONV_DIM), F32)
    s0 = jnp.zeros((batch, GDN_HEADS, GDN_DK, GDN_DV), F32)
    yb_p, d_p, c_p = _gdn(rest_p3, 0, rest_p3, R_ZB // GDN_V, ba_p.reshape(batch, seq, 128), conv0, s0,
                          conv_w[0], a_log[0], dt_bias[0], gdn_norm_g[0], GDN_CHUNK)

    y_p = _out_proj(xp2, ya_p, yb_p.reshape(batch * seq, GDN_V), ym_p.reshape(batch * seq, MEM_W), rest_p,
                    wda, wdb, wdm, wout, final_norm_g, 256)

    rows = SAMPLE_ROWS
    xs2 = jnp.pad(x_sample, ((0, 0), (0, rows - n_new), (0, 0))).reshape(bd * rows, d)
    hs = _rmsnorm(xs2, norm_g[0], BF16, bd * rows)
    q_s, kv_s_f32, _, rest_s, ba_s = _project(hs, w_q, w_kv, w_rest, w_ba, bd * rows)
    kv_s3 = kv_s_f32.reshape(bd, rows, 2 * DA_HEADS * DA_HEAD_W)
    k_s3 = kv_s3[:, :, :DA_HEADS * DA_HEAD_W]
    v_s3 = kv_s3[:, :, DA_HEADS * DA_HEAD_W:]
    rest_s3 = rest_s.reshape(bd, rows, R_WIDTH)

    q5 = q_s.reshape(bd, rows, DA_HEADS, 2, DA_HEAD_DIM)[:, :n_new]
    q5 = jnp.transpose(q5, (0, 2, 3, 1, 4))
    qblk = q5[:, :, :, :, None, :] * jnp.eye(2, dtype=BF16)[None, None, :, None, :, None]
    qblk = qblk.reshape(bd, DA_HEADS, 2 * n_new, DA_HEAD_W)
    n_phys = cache_k.shape[1]
    ck = cache_k[0].reshape(n_phys, PAGE_SIZE, DA_HEADS * DA_HEAD_W)
    cv = cache_v[0].reshape(n_phys, PAGE_SIZE, DA_HEADS * DA_HEAD_W)
    ya_s = _attn_paged(qblk, k_s3, v_s3, rest_s3, sg, lam_vecs, ck, cv, page_table, n_new, 4)

    mk_s = cache_mem_k[0].reshape(bd, MEM_LEN, MEM_W)
    mv_s = cache_mem_v[0].reshape(bd, MEM_LEN, MEM_W)
    ym_s = _mem_attn(rest_s3, mk_s, mv_s, rows)

    pad_c = ((0, 0), (0, GDN_CHUNK - rows), (0, 0))
    qkv_s = jnp.pad(rest_s3[:, :, R_QKV:R_QKV + GDN_CONV_DIM], pad_c)
    zb_s = jnp.pad(rest_s3[:, :, R_ZB:R_ZB + GDN_V], pad_c)
    ba_s3 = jnp.pad(ba_s.reshape(bd, rows, 128), pad_c)
    yb_s, d_s, c_s = _gdn(qkv_s, 0, zb_s, 0, ba_s3, state_conv[0].astype(F32), state_delta[0].astype(F32),
                          conv_w[0], a_log[0], dt_bias[0], gdn_norm_g[0], n_new)
    yb_s2 = yb_s[:, :rows].reshape(bd * rows, GDN_V)

    y_s = _out_proj(xs2, ya_s.reshape(bd * rows, d), yb_s2, ym_s.reshape(bd * rows, MEM_W), rest_s,
                    wda, wdb, wdm, wout, final_norm_g, bd * rows)

    kw = DA_HEADS * DA_HEAD_W
    y_prompt = y_p.reshape(batch, seq, d)
    y_sample = y_s.reshape(bd, rows, d)[:, :n_new]
    new_k_prompt = kv_p_f32[:, :kw].reshape(1, batch, seq, DA_HEADS, 2, DA_HEAD_DIM)
    new_v_prompt = kv_p_f32[:, kw:].reshape(1, batch, seq, DA_HEADS, DA_HEAD_W)
    new_k_sample = k_s3[:, :n_new].reshape(1, bd, n_new, DA_HEADS, 2, DA_HEAD_DIM)
    new_v_sample = v_s3[:, :n_new].reshape(1, bd, n_new, DA_HEADS, DA_HEAD_W)
    new_mem_k = mk_p.reshape(1, batch, MEM_LEN, MEM_HEADS, MEM_HEAD_DIM)
    new_mem_v = mv_p.reshape(1, batch, MEM_LEN, MEM_HEADS, MEM_HEAD_DIM)
    return (y_prompt, y_sample, new_k_prompt, new_v_prompt, new_k_sample, new_v_sample,
            new_mem_k, new_mem_v, d_p[None], d_s[None], c_p[None], c_s[None])
```

```python
import functools
import math

import jax
import jax.numpy as jnp
import numpy as np
from jax import lax
from jax.experimental import pallas as pl
from jax.experimental.pallas import tpu as pltpu

F32 = jnp.float32
BF16 = jnp.bfloat16

LANES = 128
D_MODEL = 2048
PAGE_SIZE = 128
DA_HEAD_DIM = 128
DA_HEADS = 8
DA_HEAD_W = 2 * DA_HEAD_DIM
GDN_HEADS = 8
GDN_DK = 128
GDN_DV = 256
GDN_QK = GDN_HEADS * GDN_DK
GDN_V = GDN_HEADS * GDN_DV
GDN_CONV_DIM = 2 * GDN_QK + GDN_V
CONV_WIDTH = 4
GDN_CHUNK = 64
GDN_GROUP = 4
MEM_LEN = 256
MEM_HEADS = 4
MEM_HEAD_DIM = 128
MEM_W = MEM_HEADS * MEM_HEAD_DIM
NORM_EPS = 1e-6
SUBLN_EPS = 1e-5
LAMBDA_INIT = 0.8 - 0.6 * math.exp(-0.3 * 0)
NEW_POSITIONS = LANES // DA_HEADS
SAMPLE_ROWS = 8

R_QKV = 0
R_ZA = 4096
R_ZB = 6144
R_GA = 8192
R_GB = 10240
R_GM = 12288
R_QM = 14336
R_ZM = 14848
R_WIDTH = 15360

VMEM_LIMIT = 56 * 1024 * 1024


def _cparams(sem):
    return pltpu.CompilerParams(dimension_semantics=sem, vmem_limit_bytes=VMEM_LIMIT)


def _silu(x):
    return x * jax.nn.sigmoid(x)


def _dot(a, b):
    return jnp.dot(a.astype(BF16), b.astype(BF16), preferred_element_type=F32)


def _dot_nt(a, b):
    return lax.dot_general(a.astype(BF16), b.astype(BF16), (((1,), (1,)), ((), ())),
                           preferred_element_type=F32)


def _rmsnorm_kernel(x_ref, g_ref, o_ref, *, eps):
    x = x_ref[...].astype(F32)
    y = x * lax.rsqrt(jnp.mean(x * x, axis=-1, keepdims=True) + eps)
    o_ref[...] = (y * g_ref[...]).astype(o_ref.dtype)


def _rmsnorm(x2d, g, out_dtype, tm):
    m, d = x2d.shape
    tm = min(tm, m)
    return pl.pallas_call(
        functools.partial(_rmsnorm_kernel, eps=NORM_EPS),
        grid=(m // tm,),
        in_specs=[pl.BlockSpec((tm, d), lambda i: (i, 0)),
                  pl.BlockSpec((1, d), lambda i: (0, 0))],
        out_specs=pl.BlockSpec((tm, d), lambda i: (i, 0)),
        out_shape=jax.ShapeDtypeStruct((m, d), out_dtype),
        compiler_params=_cparams(("parallel",)),
        name="rmsnorm",
    )(x2d, g.reshape(1, d).astype(F32))


def _matmul_kernel(h_ref, w_ref, *o_refs, scale):
    acc = jnp.dot(h_ref[...], w_ref[...], preferred_element_type=F32)
    if scale is not None:
        acc = acc * scale
    for o_ref in o_refs:
        o_ref[...] = acc.astype(o_ref.dtype)


def _matmul(h, w, out_dtypes, tm, tn, scale=None, name="matmul"):
    m, k = h.shape
    n = w.shape[1]
    tm, tn = min(tm, m), min(tn, n)
    outs = pl.pallas_call(
        functools.partial(_matmul_kernel, scale=scale),
        grid=(m // tm, n // tn),
        in_specs=[pl.BlockSpec((tm, k), lambda i, j: (i, 0)),
                  pl.BlockSpec((k, tn), lambda i, j: (0, j))],
        out_specs=[pl.BlockSpec((tm, tn), lambda i, j: (i, j)) for _ in out_dtypes],
        out_shape=[jax.ShapeDtypeStruct((m, n), dt) for dt in out_dtypes],
        compiler_params=_cparams(("parallel", "parallel")),
        name=name,
    )(h, w)
    return outs


def _diff_lambda(lq1_ref, lk1_ref, lq2_ref, lk2_ref):
    s1 = jnp.sum(lq1_ref[...] * lk1_ref[...], axis=-1, keepdims=True)
    s2 = jnp.sum(lq2_ref[...] * lk2_ref[...], axis=-1, keepdims=True)
    return jnp.exp(s1) - jnp.exp(s2) + LAMBDA_INIT


def _subln(od, g):
    y = od * lax.rsqrt(jnp.mean(od * od, axis=-1, keepdims=True) + SUBLN_EPS)
    return (y * g) * (1.0 - LAMBDA_INIT)


def _attn_prompt_kernel(qi_tab, ki_tab, q_ref, k_ref, v_ref, g_ref,
                        lq1_ref, lk1_ref, lq2_ref, lk2_ref, o_ref, m_sc, l_sc, acc_sc):
    step = pl.program_id(2)
    qi = qi_tab[step]
    ki = ki_tab[step]
    tq = q_ref.shape[0]
    lane_tiles = tq // LANES

    @pl.when(ki == 0)
    def _():
        m_sc[...] = jnp.full(m_sc.shape, -jnp.inf, F32)
        l_sc[...] = jnp.zeros(l_sc.shape, F32)
        acc_sc[...] = jnp.zeros(acc_sc.shape, F32)

    def update(masked):
        q = q_ref[...]
        k = k_ref[...]
        v = v_ref[...]
        if masked:
            row = lax.broadcasted_iota(jnp.int32, (tq, tq), 0)
            col = lax.broadcasted_iota(jnp.int32, (tq, tq), 1)
            keep = col <= row
        for c in range(2):
            lo, hi = c * DA_HEAD_DIM, (c + 1) * DA_HEAD_DIM
            s = lax.dot_general(q[:, lo:hi], k[:, lo:hi], (((1,), (1,)), ((), ())),
                                preferred_element_type=F32)
            if masked:
                s = jnp.where(keep, s, -jnp.inf)
            m_prev = m_sc[c]
            m_new = jnp.maximum(m_prev, jnp.max(s, axis=-1, keepdims=True))
            alpha = jnp.exp2(m_prev - m_new)
            p = jnp.exp2(s - pltpu.repeat(m_new, lane_tiles, axis=1))
            l_sc[c] = alpha * l_sc[c] + jnp.sum(p, axis=-1, keepdims=True)
            acc_sc[c] = (pltpu.repeat(alpha, DA_HEAD_W // LANES, axis=1) * acc_sc[c]
                         + jnp.dot(p.astype(BF16), v, preferred_element_type=F32))
            m_sc[c] = m_new

    @pl.when(ki < qi)
    def _():
        update(False)

    @pl.when(ki == qi)
    def _():
        update(True)
        lam = _diff_lambda(lq1_ref, lk1_ref, lq2_ref, lk2_ref)
        reps = DA_HEAD_W // LANES
        od = (acc_sc[0] / pltpu.repeat(l_sc[0], reps, axis=1)
              - lam * (acc_sc[1] / pltpu.repeat(l_sc[1], reps, axis=1)))
        o_ref[...] = _subln(od, g_ref[...]).astype(o_ref.dtype)


def _attn_prompt(q, kv, subln_g, lam_vecs, batch, seq, tq):
    m = batch * seq
    nq = seq // tq
    qi_tab = np.concatenate([np.full(i + 1, i, np.int32) for i in range(nq)])
    ki_tab = np.concatenate([np.arange(i + 1, dtype=np.int32) for i in range(nq)])
    n_steps = int(qi_tab.shape[0])
    vec = pl.BlockSpec((1, DA_HEAD_DIM), lambda b, h, s, qt, kt: (0, 0))
    grid_spec = pltpu.PrefetchScalarGridSpec(
        num_scalar_prefetch=2,
        grid=(batch, DA_HEADS, n_steps),
        in_specs=[
            pl.BlockSpec((tq, DA_HEAD_W), lambda b, h, s, qt, kt: (b * nq + qt[s], h)),
            pl.BlockSpec((tq, DA_HEAD_W), lambda b, h, s, qt, kt: (b * nq + kt[s], h)),
            pl.BlockSpec((tq, DA_HEAD_W), lambda b, h, s, qt, kt: (b * nq + kt[s], DA_HEADS + h)),
            pl.BlockSpec((1, DA_HEAD_W), lambda b, h, s, qt, kt: (0, 0)),
            vec, vec, vec, vec,
        ],
        out_specs=pl.BlockSpec((tq, DA_HEAD_W), lambda b, h, s, qt, kt: (b * nq + qt[s], h)),
        scratch_shapes=[pltpu.VMEM((2, tq, LANES), F32), pltpu.VMEM((2, tq, LANES), F32),
                        pltpu.VMEM((2, tq, DA_HEAD_W), F32)],
    )
    return pl.pallas_call(
        _attn_prompt_kernel,
        grid_spec=grid_spec,
        out_shape=jax.ShapeDtypeStruct((m, DA_HEADS * DA_HEAD_W), BF16),
        compiler_params=_cparams(("parallel", "parallel", "arbitrary")),
        name="attn_prompt",
    )(jnp.asarray(qi_tab), jnp.asarray(ki_tab), q, kv, kv, subln_g, *lam_vecs)


def _attn_paged_kernel(pt_ref, q_ref, kn_ref, vn_ref, g_ref, lq1_ref, lk1_ref, lq2_ref, lk2_ref,
                       *refs, pages_per_step):
    k_refs = refs[:pages_per_step]
    v_refs = refs[pages_per_step:2 * pages_per_step]
    o_ref = refs[2 * pages_per_step]
    m_sc, l_sc, acc_sc = refs[2 * pages_per_step + 1:]
    j = pl.program_id(1)
    nq = q_ref.shape[2]
    head_bits = DA_HEADS.bit_length() - 1

    def same_head(ncols):
        row = lax.broadcasted_iota(jnp.int32, (nq, ncols), 0)
        col = lax.broadcasted_iota(jnp.int32, (nq, ncols), 1)
        return row, col, (row & (DA_HEADS - 1)) == (col & (DA_HEADS - 1))

    def update(k_ref, v_ref, keep):
        ncols = v_ref.shape[1]
        ps, alphas = [], []
        for c in range(2):
            k_c = k_ref[0, pl.ds(c, ncols, stride=2), :].astype(BF16)
            s = lax.dot_general(q_ref[0, c], k_c, (((1,), (1,)), ((), ())), preferred_element_type=F32)
            s = jnp.where(keep, s, -jnp.inf)
            rows = pl.ds(c * nq, nq)
            m_prev = m_sc[rows, :]
            m_new = jnp.maximum(m_prev, jnp.max(s, axis=-1, keepdims=True))
            alpha = jnp.exp2(m_prev - m_new)
            p = jnp.exp2(s - m_new)
            l_sc[rows, :] = alpha * l_sc[rows, :] + jnp.sum(p, axis=-1, keepdims=True)
            m_sc[rows, :] = m_new
            ps.append(p.astype(BF16))
            alphas.append(alpha)
        pv = jnp.dot(jnp.concatenate(ps, axis=0), v_ref[0].astype(BF16), preferred_element_type=F32)
        acc_sc[...] = jnp.concatenate(alphas, axis=0) * acc_sc[...] + pv

    @pl.when(j == 0)
    def _():
        m_sc[...] = jnp.full(m_sc.shape, -jnp.inf, F32)
        l_sc[...] = jnp.zeros(l_sc.shape, F32)
        acc_sc[...] = jnp.zeros(acc_sc.shape, F32)
        row, col, keep = same_head(vn_ref.shape[1])
        keep = keep & ((col >> head_bits) <= (row >> head_bits))
        update(kn_ref, vn_ref, keep)

    _, _, keep_page = same_head(PAGE_SIZE * DA_HEADS)
    for r in range(pages_per_step):
        update(k_refs[r], v_refs[r], keep_page)

    @pl.when(j == pl.num_programs(1) - 1)
    def _():
        lam = _diff_lambda(lq1_ref, lk1_ref, lq2_ref, lk2_ref)
        o = acc_sc[...] / l_sc[...]
        od = o[0:nq] - lam * o[nq:2 * nq]
        o_ref[0] = _subln(od, g_ref[...])


def _attn_paged(q2, k_new, v_new, subln_g, lam_vecs, cache_k, cache_v, page_table, pages_per_step):
    bd, n_pages = page_table.shape
    n_steps = n_pages // pages_per_step
    nq = q2.shape[2]
    pt_flat = page_table.reshape(-1)
    page = lambda b, j, pt, r: (pt[b * n_pages + j * pages_per_step + r], 0, 0)
    k_specs = [pl.BlockSpec((1,) + cache_k.shape[1:], functools.partial(page, r=r)) for r in range(pages_per_step)]
    v_specs = [pl.BlockSpec((1,) + cache_v.shape[1:], functools.partial(page, r=r)) for r in range(pages_per_step)]
    vec = pl.BlockSpec((1, DA_HEAD_DIM), lambda b, j, pt: (0, 0))
    per_seq = lambda a: pl.BlockSpec((1,) + a.shape[1:], lambda b, j, pt: (b,) + (0,) * (a.ndim - 1))
    grid_spec = pltpu.PrefetchScalarGridSpec(
        num_scalar_prefetch=1,
        grid=(bd, n_steps),
        in_specs=[per_seq(q2), per_seq(k_new), per_seq(v_new),
                  pl.BlockSpec((1, DA_HEAD_W), lambda b, j, pt: (0, 0)),
                  vec, vec, vec, vec] + k_specs + v_specs,
        out_specs=pl.BlockSpec((1, nq, DA_HEAD_W), lambda b, j, pt: (b, 0, 0)),
        scratch_shapes=[pltpu.VMEM((2 * nq, 1), F32), pltpu.VMEM((2 * nq, 1), F32),
                        pltpu.VMEM((2 * nq, DA_HEAD_W), F32)],
    )
    return pl.pallas_call(
        functools.partial(_attn_paged_kernel, pages_per_step=pages_per_step),
        grid_spec=grid_spec,
        out_shape=jax.ShapeDtypeStruct((bd, nq, DA_HEAD_W), F32),
        compiler_params=_cparams(("parallel", "arbitrary")),
        name="attn_paged",
    )(pt_flat, q2, k_new, v_new, subln_g, *lam_vecs,
      *([cache_k] * pages_per_step), *([cache_v] * pages_per_step))


def _mem_attn_kernel(q_ref, z_ref, mk_ref, mv_ref, o_ref):
    scale = MEM_HEAD_DIM ** -0.5
    for h in range(MEM_HEADS):
        lo, hi = h * MEM_HEAD_DIM, (h + 1) * MEM_HEAD_DIM
        s = _dot_nt(q_ref[0, :, lo:hi], mk_ref[0, :, lo:hi]) * scale
        s = s - jnp.max(s, axis=-1, keepdims=True)
        e = jnp.exp(s)
        p = e / jnp.sum(e, axis=-1, keepdims=True)
        o = _dot(p, mv_ref[0, :, lo:hi])
        o_ref[0, :, lo:hi] = (o * _silu(z_ref[0, :, lo:hi].astype(F32))).astype(o_ref.dtype)


def _mem_attn(rest3, mk, mv, tm):
    b, length, _ = rest3.shape
    tm = min(tm, length)
    return pl.pallas_call(
        _mem_attn_kernel,
        grid=(b, length // tm),
        in_specs=[pl.BlockSpec((1, tm, MEM_W), lambda i, j: (i, j, R_QM // MEM_W)),
                  pl.BlockSpec((1, tm, MEM_W), lambda i, j: (i, j, R_ZM // MEM_W)),
                  pl.BlockSpec((1, MEM_LEN, MEM_W), lambda i, j: (i, 0, 0)),
                  pl.BlockSpec((1, MEM_LEN, MEM_W), lambda i, j: (i, 0, 0))],
        out_specs=pl.BlockSpec((1, tm, MEM_W), lambda i, j: (i, j, 0)),
        out_shape=jax.ShapeDtypeStruct((b, length, MEM_W), BF16),
        compiler_params=_cparams(("parallel", "parallel")),
        name="mem_attn",
    )(rest3, rest3, mk, mv)


def _softplus(x):
    return jnp.maximum(x, 0.0) + jnp.log1p(jnp.exp(-jnp.abs(x)))


def _l2n(x):
    return x * lax.rsqrt(jnp.sum(x * x, axis=-1, keepdims=True) + 1e-6)


def _gdn_kernel(qkv_ref, zb_ref, ba_ref, conv0_ref, s0_ref, convw_ref, alog_ref, dtb_ref, gng_ref,
                yb_ref, sout_ref, convout_ref, xp_sc, s_sc, *, valid_len):
    i = pl.program_id(1)
    c = GDN_CHUNK
    gw = GDN_GROUP * c
    tail = CONV_WIDTH - 1

    @pl.when(i == 0)
    def _():
        xp_sc[8 - tail:8, :] = conv0_ref[0]
        s_sc[...] = s0_ref[0]

    xp_sc[8:8 + c, :] = qkv_ref[0].astype(F32)
    w = convw_ref[...]
    y = xp_sc[8 - tail:8 - tail + c, :] * w[0:1]
    for t in range(1, CONV_WIDTH):
        y = y + xp_sc[8 - tail + t:8 - tail + t + c, :] * w[t:t + 1]
    qkv = _silu(y)

    @pl.when(i == pl.num_programs(1) - 1)
    def _():
        convout_ref[0] = xp_sc[8 + valid_len - tail:8 + valid_len, :]

    xp_sc[8 - tail:8, :] = xp_sc[8 + c - tail:8 + c, :]

    ba = ba_ref[0]
    if valid_len < c:
        live = lax.broadcasted_iota(jnp.int32, (c, 1), 0) < valid_len

    ii = lax.broadcasted_iota(jnp.int32, (gw, gw), 0)
    jj = lax.broadcasted_iota(jnp.int32, (gw, gw), 1)
    shift = int(math.log2(c))
    same = lax.shift_right_logical(ii, shift) == lax.shift_right_logical(jj, shift)
    incl = same & (jj <= ii)
    strict = same & (jj < ii)
    upper = same & (ii <= jj)
    eye = ii == jj

    for grp in range(GDN_HEADS // GDN_GROUP):
        heads = [grp * GDN_GROUP + t for t in range(GDN_GROUP)]
        qs, ks, vs, bcols, gcols = [], [], [], [], []
        for h in heads:
            qs.append(_l2n(qkv[:, h * GDN_DK:(h + 1) * GDN_DK]) * (GDN_DK ** -0.5))
            ks.append(_l2n(qkv[:, GDN_QK + h * GDN_DK:GDN_QK + (h + 1) * GDN_DK]))
            vs.append(qkv[:, 2 * GDN_QK + h * GDN_DV:2 * GDN_QK + (h + 1) * GDN_DV])
            beta = jax.nn.sigmoid(ba[:, h:h + 1])
            g = -jnp.exp(alog_ref[0:1, h:h + 1]) * _softplus(
                ba[:, GDN_HEADS + h:GDN_HEADS + h + 1] + dtb_ref[0:1, h:h + 1])
            if valid_len < c:
                beta = jnp.where(live, beta, 0.0)
                g = jnp.where(live, g, 0.0)
            bcols.append(beta)
            gcols.append(g)
        q_st = jnp.concatenate(qs, axis=0)
        k_st = jnp.concatenate(ks, axis=0)
        v_st = jnp.concatenate(vs, axis=0)
        bcol = jnp.concatenate(bcols, axis=0)
        gcol = jnp.concatenate(gcols, axis=0)

        grow = jnp.sum(jnp.where(eye, gcol, 0.0), axis=0, keepdims=True)
        gc_col = jnp.sum(jnp.where(incl, grow, 0.0), axis=1, keepdims=True)
        gc_row = jnp.sum(jnp.where(upper, gcol, 0.0), axis=0, keepdims=True)
        dmat = jnp.exp(jnp.where(incl, gc_col - gc_row, -jnp.inf))
        egc = jnp.exp(gc_col)

        kk = _dot_nt(k_st, k_st)
        pw = -jnp.where(strict, bcol * kk * dmat, 0.0)
        yinv = pw
        for _ in range(int(math.log2(c)) - 1):
            pw = _dot(pw, pw)
            yinv = yinv + pw + _dot(yinv, pw)

        rhs_v = bcol * v_st
        rhs_k = (bcol * egc) * k_st
        uv = rhs_v + _dot(yinv, rhs_v)
        wk = rhs_k + _dot(yinv, rhs_k)
        aqk = _dot_nt(q_st, k_st) * dmat

        us, qss, s_olds = [], [], []
        for t, h in enumerate(heads):
            r0, r1 = t * c, (t + 1) * c
            s_old = s_sc[h]
            res = _dot(jnp.concatenate([wk[r0:r1], q_st[r0:r1]], axis=0), s_old)
            us.append(uv[r0:r1] - res[0:c])
            qss.append(res[c:2 * c])
            s_olds.append(s_old)
        au = _dot(aqk, jnp.concatenate(us, axis=0))
        for t, h in enumerate(heads):
            r0, r1 = t * c, (t + 1) * c
            o = egc[r0:r1] * qss[t] + au[r0:r1]
            gc_last = gc_col[r1 - 1:r1]
            kdec = k_st[r0:r1] * jnp.exp(gc_last - gc_col[r0:r1])
            s_sc[h] = jnp.exp(gc_last) * s_olds[t] + _dot(kdec.T, us[t])
            on = o * lax.rsqrt(jnp.mean(o * o, axis=-1, keepdims=True) + NORM_EPS) * gng_ref[...]
            z = zb_ref[0, :, h * GDN_DV:(h + 1) * GDN_DV].astype(F32)
            yb_ref[0, :, h * GDN_DV:(h + 1) * GDN_DV] = (on * _silu(z)).astype(yb_ref.dtype)

    @pl.when(i == pl.num_programs(1) - 1)
    def _():
        sout_ref[0] = s_sc[...]


def _gdn(qkv3, qkv_blk, zb3, zb_blk, ba3, conv0, s0, conv_w, a_log, dt_bias, gng, valid_len):
    b, length = ba3.shape[0], ba3.shape[1]
    c = GDN_CHUNK
    n = length // c
    tail = CONV_WIDTH - 1
    const = lambda *shape: pl.BlockSpec(shape, lambda bi, i: (0,) * len(shape))
    return pl.pallas_call(
        functools.partial(_gdn_kernel, valid_len=valid_len),
        grid=(b, n),
        in_specs=[
            pl.BlockSpec((1, c, GDN_CONV_DIM), lambda bi, i: (bi, i, qkv_blk)),
            pl.BlockSpec((1, c, GDN_V), lambda bi, i: (bi, i, zb_blk)),
            pl.BlockSpec((1, c, 128), lambda bi, i: (bi, i, 0)),
            pl.BlockSpec((1, tail, GDN_CONV_DIM), lambda bi, i: (bi, 0, 0)),
            pl.BlockSpec((1, GDN_HEADS, GDN_DK, GDN_DV), lambda bi, i: (bi, 0, 0, 0)),
            const(CONV_WIDTH, GDN_CONV_DIM), const(1, GDN_HEADS), const(1, GDN_HEADS), const(1, GDN_DV),
        ],
        out_specs=[
            pl.BlockSpec((1, c, GDN_V), lambda bi, i: (bi, i, 0)),
            pl.BlockSpec((1, GDN_HEADS, GDN_DK, GDN_DV), lambda bi, i: (bi, 0, 0, 0)),
            pl.BlockSpec((1, tail, GDN_CONV_DIM), lambda bi, i: (bi, 0, 0)),
        ],
        out_shape=[
            jax.ShapeDtypeStruct((b, length, GDN_V), BF16),
            jax.ShapeDtypeStruct((b, GDN_HEADS, GDN_DK, GDN_DV), F32),
            jax.ShapeDtypeStruct((b, tail, GDN_CONV_DIM), F32),
        ],
        scratch_shapes=[pltpu.VMEM((8 + c, GDN_CONV_DIM), F32),
                        pltpu.VMEM((GDN_HEADS, GDN_DK, GDN_DV), F32)],
        compiler_params=_cparams(("parallel", "arbitrary")),
        name="gdn",
    )(qkv3, zb3, ba3, conv0, s0, conv_w.astype(F32), a_log.reshape(1, -1).astype(F32),
      dt_bias.reshape(1, -1).astype(F32), gng.reshape(1, -1).astype(F32))


def _out_kernel(x_ref, oa_ref, za_ref, yb_ref, ym_ref, ga_ref, gb_ref, gm_ref,
                wda_ref, wdb_ref, wdm_ref, wout_ref, fg_ref, o_ref):
    sig = lambda r: jax.nn.sigmoid(r[...].astype(F32))
    ya = oa_ref[...].astype(F32) * _silu(za_ref[...].astype(F32))
    merged = sig(ga_ref) * _dot(ya, wda_ref[...])
    merged = merged + sig(gb_ref) * _dot(yb_ref[...], wdb_ref[...])
    merged = merged + sig(gm_ref) * _dot(ym_ref[...], wdm_ref[...])
    r = x_ref[...] + _dot(merged, wout_ref[...])
    y = r * lax.rsqrt(jnp.mean(r * r, axis=-1, keepdims=True) + NORM_EPS)
    o_ref[...] = y * fg_ref[...]


def _out_proj(x2d, oa, yb, ym, rest, wda, wdb, wdm, wout, final_g, tm):
    m, d = x2d.shape
    tm = min(tm, m)
    row = lambda width, blk: pl.BlockSpec((tm, width), lambda i: (i, blk))
    resident = lambda shape: pl.BlockSpec(shape, lambda i: (0, 0), pipeline_mode=pl.Buffered(1))
    return pl.pallas_call(
        _out_kernel,
        grid=(m // tm,),
        in_specs=[row(d, 0), row(d, 0), row(d, R_ZA // d), row(d, 0), row(MEM_W, 0),
                  row(d, R_GA // d), row(d, R_GB // d), row(d, R_GM // d),
                  resident(wda.shape), resident(wdb.shape), resident(wdm.shape), resident(wout.shape),
                  resident((1, d))],
        out_specs=row(d, 0),
        out_shape=jax.ShapeDtypeStruct((m, d), F32),
        compiler_params=_cparams(("parallel",)),
        name="out_proj",
    )(x2d, oa, rest, yb, ym, rest, rest, rest, wda, wdb, wdm, wout, final_g.reshape(1, d).astype(F32))


def _project(h, w_q, w_kv, w_rest, w_ba, tm):
    scale = DA_HEAD_DIM ** -0.5 * math.log2(math.e)
    (q,) = _matmul(h, w_q, [BF16], tm, 1024, scale=scale, name="proj_q")
    kv_f32, kv = _matmul(h, w_kv, [F32, BF16], tm, 1024, name="proj_kv")
    (rest,) = _matmul(h, w_rest, [BF16], tm, 1024, name="proj_rest")
    (ba,) = _matmul(h, w_ba, [F32], tm, 128, name="proj_ba")
    return q, kv_f32, kv, rest, ba


def kernel(x_prompt, x_sample, cache_k, cache_v, cache_mem_k, cache_mem_v, state_delta, state_conv, page_table, mem_prompt, norm_g, w_in, lambda_q1, lambda_k1, lambda_q2, lambda_k2, subln_g, conv_w, a_log, dt_bias, gdn_norm_g, w_down_a, w_down_b, w_down_m, w_out, mem_norm_g, w_mem_kv, final_norm_g):
    assert w_in.shape[0] == 1, "single-layer trunk"
    batch, seq, d = x_prompt.shape
    bd, n_new, _ = x_sample.shape
    assert d == D_MODEL and 2 * n_new == SAMPLE_ROWS and n_new >= CONV_WIDTH - 1
    assert seq % 512 == 0 and seq % GDN_CHUNK == 0

    w = w_in[0]
    o_qa, o_ka, o_za, o_qb, o_zb, o_b, o_qm, o_ga = 0, 2048, 6144, 8192, 12288, 14336, 14352, 15376
    w_q = w[:, o_qa:o_ka].astype(BF16)
    w_kv = w[:, o_ka:o_za].astype(BF16)
    w_rest = jnp.concatenate([w[:, o_qb:o_zb], w[:, o_za:o_qb], w[:, o_zb:o_b], w[:, o_ga:], w[:, o_qm:o_ga]],
                             axis=1).astype(BF16)
    w_ba = jnp.pad(w[:, o_b:o_qm], ((0, 0), (0, 128 - 2 * GDN_HEADS))).astype(BF16)
    wda, wdb, wdm, wout = (t[0].astype(BF16) for t in (w_down_a, w_down_b, w_down_m, w_out))
    lam_vecs = [t[0].reshape(1, DA_HEAD_DIM).astype(F32) for t in (lambda_q1, lambda_k1, lambda_q2, lambda_k2)]
    sg = subln_g[0].reshape(1, DA_HEAD_W).astype(F32)

    xp2 = x_prompt.reshape(batch * seq, d)
    hp = _rmsnorm(xp2, norm_g[0], BF16, 512)
    q_p, kv_p_f32, kv_p, rest_p, ba_p = _project(hp, w_q, w_kv, w_rest, w_ba, 1024)
    oa_p = _attn_prompt(q_p, kv_p, sg, lam_vecs, batch, seq, 512)

    hm = _rmsnorm(mem_prompt.reshape(batch * MEM_LEN, d), mem_norm_g[0], BF16, 512)
    (mem_kv,) = _matmul(hm, w_mem_kv[0].astype(BF16), [F32], 512, 1024, name="proj_mem_kv")
    mk_p = mem_kv[:, :MEM_W].reshape(batch, MEM_LEN, MEM_W)
    mv_p = mem_kv[:, MEM_W:].reshape(batch, MEM_LEN, MEM_W)
    rest_p3 = rest_p.reshape(batch, seq, R_WIDTH)
    ym_p = _mem_attn(rest_p3, mk_p, mv_p, 512)

    conv0 = jnp.zeros((batch, CONV_WIDTH - 1, GDN_CONV_DIM), F32)
    s0 = jnp.zeros((batch, GDN_HEADS, GDN_DK, GDN_DV), F32)
    yb_p, d_p, c_p = _gdn(rest_p3, 0, rest_p3, R_ZB // GDN_V, ba_p.reshape(batch, seq, 128), conv0, s0,
                          conv_w[0], a_log[0], dt_bias[0], gdn_norm_g[0], GDN_CHUNK)

    y_p = _out_proj(xp2, oa_p, yb_p.reshape(batch * seq, GDN_V), ym_p.reshape(batch * seq, MEM_W), rest_p,
                    wda, wdb, wdm, wout, final_norm_g, 256)

    rows = SAMPLE_ROWS
    xs2 = jnp.pad(x_sample, ((0, 0), (0, rows - n_new), (0, 0))).reshape(bd * rows, d)
    hs = _rmsnorm(xs2, norm_g[0], BF16, bd * rows)
    q_s, kv_s_f32, _, rest_s, ba_s = _project(hs, w_q, w_kv, w_rest, w_ba, bd * rows)
    kv_s3 = kv_s_f32.reshape(bd, rows, 2 * DA_HEADS * DA_HEAD_W)
    k_s3 = kv_s3[:, :, :DA_HEADS * DA_HEAD_W]
    v_s3 = kv_s3[:, :, DA_HEADS * DA_HEAD_W:]
    rest_s3 = rest_s.reshape(bd, rows, R_WIDTH)

    q2 = q_s.reshape(bd, rows, DA_HEADS, 2, DA_HEAD_DIM)[:, :n_new]
    q2 = jnp.transpose(q2, (0, 3, 1, 2, 4)).reshape(bd, 2, n_new * DA_HEADS, DA_HEAD_DIM)
    pad_new = ((0, 0), (0, NEW_POSITIONS - rows), (0, 0))
    k_new = jnp.pad(k_s3, pad_new).reshape(bd, NEW_POSITIONS * DA_HEADS * 2, DA_HEAD_DIM)
    v_new = jnp.pad(v_s3, pad_new).reshape(bd, NEW_POSITIONS * DA_HEADS, DA_HEAD_W)
    n_phys = cache_k.shape[1]
    ck = cache_k.reshape(n_phys, PAGE_SIZE * DA_HEADS * 2, DA_HEAD_DIM)
    cv = cache_v.reshape(n_phys, PAGE_SIZE * DA_HEADS, DA_HEAD_W)
    oa_s = _attn_paged(q2, k_new, v_new, sg, lam_vecs, ck, cv, page_table, 4)
    oa_s = jnp.pad(oa_s.reshape(bd, n_new, d), ((0, 0), (0, rows - n_new), (0, 0)))

    mk_s = cache_mem_k[0].reshape(bd, MEM_LEN, MEM_W)
    mv_s = cache_mem_v[0].reshape(bd, MEM_LEN, MEM_W)
    ym_s = _mem_attn(rest_s3, mk_s, mv_s, rows)

    pad_c = ((0, 0), (0, GDN_CHUNK - rows), (0, 0))
    qkv_s = jnp.pad(rest_s3[:, :, R_QKV:R_QKV + GDN_CONV_DIM], pad_c)
    zb_s = jnp.pad(rest_s3[:, :, R_ZB:R_ZB + GDN_V], pad_c)
    ba_s3 = jnp.pad(ba_s.reshape(bd, rows, 128), pad_c)
    yb_s, d_s, c_s = _gdn(qkv_s, 0, zb_s, 0, ba_s3, state_conv[0].astype(F32), state_delta[0].astype(F32),
                          conv_w[0], a_log[0], dt_bias[0], gdn_norm_g[0], n_new)
    yb_s2 = yb_s[:, :rows].reshape(bd * rows, GDN_V)

    y_s = _out_proj(xs2, oa_s.reshape(bd * rows, d), yb_s2, ym_s.reshape(bd * rows, MEM_W), rest_s,
                    wda, wdb, wdm, wout, final_norm_g, bd * rows)

    kw = DA_HEADS * DA_HEAD_W
    y_prompt = y_p.reshape(batch, seq, d)
    y_sample = y_s.reshape(bd, rows, d)[:, :n_new]
    new_k_prompt = kv_p_f32[:, :kw].reshape(1, batch, seq, DA_HEADS, 2, DA_HEAD_DIM)
    new_v_prompt = kv_p_f32[:, kw:].reshape(1, batch, seq, DA_HEADS, DA_HEAD_W)
    new_k_sample = k_s3[:, :n_new].reshape(1, bd, n_new, DA_HEADS, 2, DA_HEAD_DIM)
    new_v_sample = v_s3[:, :n_new].reshape(1, bd, n_new, DA_HEADS, DA_HEAD_W)
    new_mem_k = mk_p.reshape(1, batch, MEM_LEN, MEM_HEADS, MEM_HEAD_DIM)
    new_mem_v = mv_p.reshape(1, batch, MEM_LEN, MEM_HEADS, MEM_HEAD_DIM)
    return (y_prompt, y_sample, new_k_prompt, new_v_prompt, new_k_sample, new_v_sample,
            new_mem_k, new_mem_v, d_p[None], d_s[None], c_p[None], c_s[None])
```

```python
import functools
import math

import jax
import jax.numpy as jnp
import numpy as np
from jax import lax
from jax.experimental import pallas as pl
from jax.experimental.pallas import tpu as pltpu

F32 = jnp.float32
BF16 = jnp.bfloat16

LANES = 128
D_MODEL = 2048
PAGE_SIZE = 128
DA_HEAD_DIM = 128
DA_HEADS = 8
DA_HEAD_W = 2 * DA_HEAD_DIM
GDN_HEADS = 8
GDN_DK = 128
GDN_DV = 256
GDN_QK = GDN_HEADS * GDN_DK
GDN_V = GDN_HEADS * GDN_DV
GDN_CONV_DIM = 2 * GDN_QK + GDN_V
CONV_WIDTH = 4
GDN_CHUNK = 64
GDN_GROUP = 4
MEM_LEN = 256
MEM_HEADS = 4
MEM_HEAD_DIM = 128
MEM_W = MEM_HEADS * MEM_HEAD_DIM
NORM_EPS = 1e-6
SUBLN_EPS = 1e-5
LAMBDA_INIT = 0.8 - 0.6 * math.exp(-0.3 * 0)
NEW_POSITIONS = LANES // DA_HEADS
SAMPLE_ROWS = 8

R_QKV = 0
R_ZA = 4096
R_ZB = 6144
R_GA = 8192
R_GB = 10240
R_GM = 12288
R_QM = 14336
R_ZM = 14848
R_WIDTH = 15360

VMEM_LIMIT = 56 * 1024 * 1024


def _cparams(sem):
    return pltpu.CompilerParams(dimension_semantics=sem, vmem_limit_bytes=VMEM_LIMIT)


def _sigmoid(x):
    return 0.5 * jnp.tanh(0.5 * x) + 0.5


def _silu(x):
    return x * _sigmoid(x)


def _dot(a, b):
    return jnp.dot(a.astype(BF16), b.astype(BF16), preferred_element_type=F32)


def _dot_nt(a, b):
    return lax.dot_general(a.astype(BF16), b.astype(BF16), (((1,), (1,)), ((), ())),
                           preferred_element_type=F32)


def _rmsnorm_kernel(x_ref, g_ref, o_ref, *, eps):
    x = x_ref[...].astype(F32)
    y = x * lax.rsqrt(jnp.mean(x * x, axis=-1, keepdims=True) + eps)
    o_ref[...] = (y * g_ref[...]).astype(o_ref.dtype)


def _rmsnorm(x2d, g, out_dtype, tm):
    m, d = x2d.shape
    tm = min(tm, m)
    return pl.pallas_call(
        functools.partial(_rmsnorm_kernel, eps=NORM_EPS),
        grid=(m // tm,),
        in_specs=[pl.BlockSpec((tm, d), lambda i: (i, 0)),
                  pl.BlockSpec((1, d), lambda i: (0, 0))],
        out_specs=pl.BlockSpec((tm, d), lambda i: (i, 0)),
        out_shape=jax.ShapeDtypeStruct((m, d), out_dtype),
        compiler_params=_cparams(("parallel",)),
        name="rmsnorm",
    )(x2d, g.reshape(1, d).astype(F32))


def _matmul_kernel(h_ref, w_ref, *o_refs, scale):
    acc = jnp.dot(h_ref[...], w_ref[...], preferred_element_type=F32)
    if scale is not None:
        acc = acc * scale
    for o_ref in o_refs:
        o_ref[...] = acc.astype(o_ref.dtype)


def _matmul(h, w, out_dtypes, tm, tn, scale=None, name="matmul"):
    m, k = h.shape
    n = w.shape[1]
    tm, tn = min(tm, m), min(tn, n)
    outs = pl.pallas_call(
        functools.partial(_matmul_kernel, scale=scale),
        grid=(m // tm, n // tn),
        in_specs=[pl.BlockSpec((tm, k), lambda i, j: (i, 0)),
                  pl.BlockSpec((k, tn), lambda i, j: (0, j))],
        out_specs=[pl.BlockSpec((tm, tn), lambda i, j: (i, j)) for _ in out_dtypes],
        out_shape=[jax.ShapeDtypeStruct((m, n), dt) for dt in out_dtypes],
        compiler_params=_cparams(("parallel", "parallel")),
        name=name,
    )(h, w)
    return outs


def _proj_k_kernel(h_ref, w_ref, k_ref, kb_ref, ba_ref):
    acc = jnp.dot(h_ref[...], w_ref[...], preferred_element_type=F32)
    n = k_ref.shape[1]
    k_ref[...] = acc[:, :n]
    kb_ref[...] = acc[:, :n].astype(kb_ref.dtype)
    ba_ref[...] = acc[:, n:]


def _proj_k(h, w_kba, tm):
    m, k = h.shape
    n = w_kba.shape[1] - LANES
    tm = min(tm, m)
    return pl.pallas_call(
        _proj_k_kernel,
        grid=(m // tm,),
        in_specs=[pl.BlockSpec((tm, k), lambda i: (i, 0)),
                  pl.BlockSpec(w_kba.shape, lambda i: (0, 0), pipeline_mode=pl.Buffered(1))],
        out_specs=[pl.BlockSpec((tm, n), lambda i: (i, 0)), pl.BlockSpec((tm, n), lambda i: (i, 0)),
                   pl.BlockSpec((tm, LANES), lambda i: (i, 0))],
        out_shape=[jax.ShapeDtypeStruct((m, n), F32), jax.ShapeDtypeStruct((m, n), BF16),
                   jax.ShapeDtypeStruct((m, LANES), F32)],
        compiler_params=_cparams(("parallel",)),
        name="proj_k",
    )(h, w_kba)


def _diff_lambda(lq1_ref, lk1_ref, lq2_ref, lk2_ref):
    s1 = jnp.sum(lq1_ref[...] * lk1_ref[...], axis=-1, keepdims=True)
    s2 = jnp.sum(lq2_ref[...] * lk2_ref[...], axis=-1, keepdims=True)
    return jnp.exp(s1) - jnp.exp(s2) + LAMBDA_INIT


def _subln(od, g):
    y = od * lax.rsqrt(jnp.mean(od * od, axis=-1, keepdims=True) + SUBLN_EPS)
    return (y * g) * (1.0 - LAMBDA_INIT)


def _attn_prompt_kernel(qi_tab, ki_tab, q_ref, k_ref, v_ref, g_ref,
                        lq1_ref, lk1_ref, lq2_ref, lk2_ref, o_ref, m_sc, l_sc, acc_sc):
    step = pl.program_id(2)
    qi = qi_tab[step]
    ki = ki_tab[step]
    tq = q_ref.shape[0]
    lane_tiles = tq // LANES

    @pl.when(ki == 0)
    def _():
        m_sc[...] = jnp.full(m_sc.shape, -jnp.inf, F32)
        l_sc[...] = jnp.zeros(l_sc.shape, F32)
        acc_sc[...] = jnp.zeros(acc_sc.shape, F32)

    def update(masked):
        q = q_ref[...]
        k = k_ref[...]
        v = v_ref[...]
        if masked:
            row = lax.broadcasted_iota(jnp.int32, (tq, tq), 0)
            col = lax.broadcasted_iota(jnp.int32, (tq, tq), 1)
            keep = col <= row
        for c in range(2):
            lo, hi = c * DA_HEAD_DIM, (c + 1) * DA_HEAD_DIM
            s = lax.dot_general(q[:, lo:hi], k[:, lo:hi], (((1,), (1,)), ((), ())),
                                preferred_element_type=F32)
            if masked:
                s = jnp.where(keep, s, -jnp.inf)
            m_prev = m_sc[c]
            m_new = jnp.maximum(m_prev, jnp.max(s, axis=-1, keepdims=True))
            alpha = jnp.exp2(m_prev - m_new)
            p = jnp.exp2(s - pltpu.repeat(m_new, lane_tiles, axis=1))
            l_sc[c] = alpha * l_sc[c] + jnp.sum(p, axis=-1, keepdims=True)
            acc_sc[c] = (pltpu.repeat(alpha, DA_HEAD_W // LANES, axis=1) * acc_sc[c]
                         + jnp.dot(p.astype(BF16), v, preferred_element_type=F32))
            m_sc[c] = m_new

    @pl.when(ki < qi)
    def _():
        update(False)

    @pl.when(ki == qi)
    def _():
        update(True)
        lam = _diff_lambda(lq1_ref, lk1_ref, lq2_ref, lk2_ref)
        reps = DA_HEAD_W // LANES
        od = (acc_sc[0] / pltpu.repeat(l_sc[0], reps, axis=1)
              - lam * (acc_sc[1] / pltpu.repeat(l_sc[1], reps, axis=1)))
        o_ref[...] = _subln(od, g_ref[...]).astype(o_ref.dtype)


def _attn_prompt(q, k, v, subln_g, lam_vecs, batch, seq, tq):
    m = batch * seq
    nq = seq // tq
    qi_tab = np.concatenate([np.full(i + 1, i, np.int32) for i in range(nq)])
    ki_tab = np.concatenate([np.arange(i + 1, dtype=np.int32) for i in range(nq)])
    n_steps = int(qi_tab.shape[0])
    vec = pl.BlockSpec((1, DA_HEAD_DIM), lambda b, h, s, qt, kt: (0, 0))
    grid_spec = pltpu.PrefetchScalarGridSpec(
        num_scalar_prefetch=2,
        grid=(batch, DA_HEADS, n_steps),
        in_specs=[
            pl.BlockSpec((tq, DA_HEAD_W), lambda b, h, s, qt, kt: (b * nq + qt[s], h)),
            pl.BlockSpec((tq, DA_HEAD_W), lambda b, h, s, qt, kt: (b * nq + kt[s], h)),
            pl.BlockSpec((tq, DA_HEAD_W), lambda b, h, s, qt, kt: (b * nq + kt[s], h)),
            pl.BlockSpec((1, DA_HEAD_W), lambda b, h, s, qt, kt: (0, 0)),
            vec, vec, vec, vec,
        ],
        out_specs=pl.BlockSpec((tq, DA_HEAD_W), lambda b, h, s, qt, kt: (b * nq + qt[s], h)),
        scratch_shapes=[pltpu.VMEM((2, tq, LANES), F32), pltpu.VMEM((2, tq, LANES), F32),
                        pltpu.VMEM((2, tq, DA_HEAD_W), F32)],
    )
    return pl.pallas_call(
        _attn_prompt_kernel,
        grid_spec=grid_spec,
        out_shape=jax.ShapeDtypeStruct((m, DA_HEADS * DA_HEAD_W), BF16),
        compiler_params=_cparams(("parallel", "parallel", "arbitrary")),
        name="attn_prompt",
    )(jnp.asarray(qi_tab), jnp.asarray(ki_tab), q, k, v, subln_g, *lam_vecs)


def _attn_paged_kernel(pt_ref, q_ref, kn_ref, vn_ref, g_ref, lq1_ref, lk1_ref, lq2_ref, lk2_ref,
                       *refs, pages_per_step):
    k_refs = refs[:pages_per_step]
    v_refs = refs[pages_per_step:2 * pages_per_step]
    o_ref = refs[2 * pages_per_step]
    m_sc, l_sc, acc_sc = refs[2 * pages_per_step + 1:]
    j = pl.program_id(1)
    nq = q_ref.shape[2]
    head_bits = DA_HEADS.bit_length() - 1

    def same_head(ncols):
        row = lax.broadcasted_iota(jnp.int32, (nq, ncols), 0)
        col = lax.broadcasted_iota(jnp.int32, (nq, ncols), 1)
        return row, col, (row & (DA_HEADS - 1)) == (col & (DA_HEADS - 1))

    def update(kv_refs, keep):
        ps, alphas = [], []
        for c in range(2):
            ss = []
            for k_ref, v_ref in kv_refs:
                k_c = k_ref[0, pl.ds(c, v_ref.shape[1], stride=2), :].astype(BF16)
                s = lax.dot_general(q_ref[0, c], k_c, (((1,), (1,)), ((), ())), preferred_element_type=F32)
                ss.append(jnp.where(keep, s, -jnp.inf))
            rows = pl.ds(c * nq, nq)
            m_prev = m_sc[rows, :]
            m_new = m_prev
            for s in ss:
                m_new = jnp.maximum(m_new, jnp.max(s, axis=-1, keepdims=True))
            alpha = jnp.exp2(m_prev - m_new)
            p_c = [jnp.exp2(s - m_new) for s in ss]
            l_new = alpha * l_sc[rows, :]
            for p in p_c:
                l_new = l_new + jnp.sum(p, axis=-1, keepdims=True)
            l_sc[rows, :] = l_new
            m_sc[rows, :] = m_new
            ps.append([p.astype(BF16) for p in p_c])
            alphas.append(alpha)
        acc = jnp.concatenate(alphas, axis=0) * acc_sc[...]
        for r, (_, v_ref) in enumerate(kv_refs):
            acc = acc + jnp.dot(jnp.concatenate([ps[0][r], ps[1][r]], axis=0), v_ref[0].astype(BF16),
                                preferred_element_type=F32)
        acc_sc[...] = acc

    @pl.when(j == 0)
    def _():
        m_sc[...] = jnp.full(m_sc.shape, -jnp.inf, F32)
        l_sc[...] = jnp.zeros(l_sc.shape, F32)
        acc_sc[...] = jnp.zeros(acc_sc.shape, F32)
        row, col, keep = same_head(vn_ref.shape[1])
        keep = keep & ((col >> head_bits) <= (row >> head_bits))
        update([(kn_ref, vn_ref)], keep)

    _, _, keep_page = same_head(PAGE_SIZE * DA_HEADS)
    update(list(zip(k_refs, v_refs)), keep_page)

    @pl.when(j == pl.num_programs(1) - 1)
    def _():
        lam = _diff_lambda(lq1_ref, lk1_ref, lq2_ref, lk2_ref)
        o = acc_sc[...] / l_sc[...]
        od = o[0:nq] - lam * o[nq:2 * nq]
        o_ref[0] = _subln(od, g_ref[...])


def _attn_paged(q2, k_new, v_new, subln_g, lam_vecs, cache_k, cache_v, page_table, pages_per_step):
    bd, n_pages = page_table.shape
    n_steps = n_pages // pages_per_step
    nq = q2.shape[2]
    pt_flat = page_table.reshape(-1)
    page = lambda b, j, pt, r: (pt[b * n_pages + j * pages_per_step + r], 0, 0)
    k_specs = [pl.BlockSpec((1,) + cache_k.shape[1:], functools.partial(page, r=r)) for r in range(pages_per_step)]
    v_specs = [pl.BlockSpec((1,) + cache_v.shape[1:], functools.partial(page, r=r)) for r in range(pages_per_step)]
    vec = pl.BlockSpec((1, DA_HEAD_DIM), lambda b, j, pt: (0, 0))
    per_seq = lambda a: pl.BlockSpec((1,) + a.shape[1:], lambda b, j, pt: (b,) + (0,) * (a.ndim - 1))
    grid_spec = pltpu.PrefetchScalarGridSpec(
        num_scalar_prefetch=1,
        grid=(bd, n_steps),
        in_specs=[per_seq(q2), per_seq(k_new), per_seq(v_new),
                  pl.BlockSpec((1, DA_HEAD_W), lambda b, j, pt: (0, 0)),
                  vec, vec, vec, vec] + k_specs + v_specs,
        out_specs=pl.BlockSpec((1, nq, DA_HEAD_W), lambda b, j, pt: (b, 0, 0)),
        scratch_shapes=[pltpu.VMEM((2 * nq, 1), F32), pltpu.VMEM((2 * nq, 1), F32),
                        pltpu.VMEM((2 * nq, DA_HEAD_W), F32)],
    )
    return pl.pallas_call(
        functools.partial(_attn_paged_kernel, pages_per_step=pages_per_step),
        grid_spec=grid_spec,
        out_shape=jax.ShapeDtypeStruct((bd, nq, DA_HEAD_W), F32),
        compiler_params=_cparams(("parallel", "arbitrary")),
        name="attn_paged",
    )(pt_flat, q2, k_new, v_new, subln_g, *lam_vecs,
      *([cache_k] * pages_per_step), *([cache_v] * pages_per_step))


def _mem_attn_kernel(q_ref, z_ref, mk_ref, mv_ref, o_ref):
    scale = MEM_HEAD_DIM ** -0.5
    for h in range(MEM_HEADS):
        lo, hi = h * MEM_HEAD_DIM, (h + 1) * MEM_HEAD_DIM
        s = _dot_nt(q_ref[0, :, lo:hi], mk_ref[0, :, lo:hi]) * scale
        s = s - jnp.max(s, axis=-1, keepdims=True)
        e = jnp.exp(s)
        p = e / jnp.sum(e, axis=-1, keepdims=True)
        o = _dot(p, mv_ref[0, :, lo:hi])
        o_ref[0, :, lo:hi] = (o * _silu(z_ref[0, :, lo:hi].astype(F32))).astype(o_ref.dtype)


def _mem_attn(rest3, mk, mv, tm):
    b, length, _ = rest3.shape
    tm = min(tm, length)
    return pl.pallas_call(
        _mem_attn_kernel,
        grid=(b, length // tm),
        in_specs=[pl.BlockSpec((1, tm, MEM_W), lambda i, j: (i, j, R_QM // MEM_W)),
                  pl.BlockSpec((1, tm, MEM_W), lambda i, j: (i, j, R_ZM // MEM_W)),
                  pl.BlockSpec((1, MEM_LEN, MEM_W), lambda i, j: (i, 0, 0)),
                  pl.BlockSpec((1, MEM_LEN, MEM_W), lambda i, j: (i, 0, 0))],
        out_specs=pl.BlockSpec((1, tm, MEM_W), lambda i, j: (i, j, 0)),
        out_shape=jax.ShapeDtypeStruct((b, length, MEM_W), BF16),
        compiler_params=_cparams(("parallel", "parallel")),
        name="mem_attn",
    )(rest3, rest3, mk, mv)


def _softplus(x):
    return jnp.maximum(x, 0.0) + jnp.log1p(jnp.exp(-jnp.abs(x)))


def _l2n(x):
    return x * lax.rsqrt(jnp.sum(x * x, axis=-1, keepdims=True) + 1e-6)


def _gdn_kernel(qkv_ref, zb_ref, ba_ref, conv0_ref, s0_ref, convw_ref, alog_ref, dtb_ref, gng_ref,
                yb_ref, sout_ref, convout_ref, xp_sc, s_sc, *, valid_len):
    i = pl.program_id(1)
    c = GDN_CHUNK
    gw = GDN_GROUP * c
    tail = CONV_WIDTH - 1

    @pl.when(i == 0)
    def _():
        xp_sc[8 - tail:8, :] = conv0_ref[0]
        s_sc[...] = s0_ref[0]

    xp_sc[8:8 + c, :] = qkv_ref[0].astype(F32)

    def conv_silu(lo, width):
        y = xp_sc[8 - tail:8 - tail + c, lo:lo + width] * convw_ref[0:1, lo:lo + width]
        for t in range(1, CONV_WIDTH):
            y = y + xp_sc[8 - tail + t:8 - tail + t + c, lo:lo + width] * convw_ref[t:t + 1, lo:lo + width]
        return _silu(y)

    @pl.when(i == pl.num_programs(1) - 1)
    def _():
        convout_ref[0] = xp_sc[8 + valid_len - tail:8 + valid_len, :]

    ba = ba_ref[0]
    if valid_len < c:
        live = lax.broadcasted_iota(jnp.int32, (c, 1), 0) < valid_len

    ii = lax.broadcasted_iota(jnp.int32, (gw, gw), 0)
    jj = lax.broadcasted_iota(jnp.int32, (gw, gw), 1)
    shift = int(math.log2(c))
    same = lax.shift_right_logical(ii, shift) == lax.shift_right_logical(jj, shift)
    incl = same & (jj <= ii)
    strict = same & (jj < ii)
    upper = same & (ii <= jj)
    eye = ii == jj

    n_squarings = max(math.ceil(math.log2(valid_len)) - 1, 0)

    for grp in range(GDN_HEADS // GDN_GROUP):
        heads = [grp * GDN_GROUP + t for t in range(GDN_GROUP)]
        q_g = conv_silu(grp * GDN_GROUP * GDN_DK, GDN_GROUP * GDN_DK)
        k_g = conv_silu(GDN_QK + grp * GDN_GROUP * GDN_DK, GDN_GROUP * GDN_DK)
        v_g = conv_silu(2 * GDN_QK + grp * GDN_GROUP * GDN_DV, GDN_GROUP * GDN_DV)
        qs, ks, vs, bcols, gcols = [], [], [], [], []
        for t, h in enumerate(heads):
            qs.append(_l2n(q_g[:, t * GDN_DK:(t + 1) * GDN_DK]) * (GDN_DK ** -0.5))
            ks.append(_l2n(k_g[:, t * GDN_DK:(t + 1) * GDN_DK]))
            vs.append(v_g[:, t * GDN_DV:(t + 1) * GDN_DV])
            beta = _sigmoid(ba[:, h:h + 1])
            g = -jnp.exp(alog_ref[0:1, h:h + 1]) * _softplus(
                ba[:, GDN_HEADS + h:GDN_HEADS + h + 1] + dtb_ref[0:1, h:h + 1])
            if valid_len < c:
                beta = jnp.where(live, beta, 0.0)
                g = jnp.where(live, g, 0.0)
            bcols.append(beta)
            gcols.append(g)
        q_st = jnp.concatenate(qs, axis=0)
        k_st = jnp.concatenate(ks, axis=0)
        v_st = jnp.concatenate(vs, axis=0)
        bcol = jnp.concatenate(bcols, axis=0)
        gcol = jnp.concatenate(gcols, axis=0)

        grow = jnp.sum(jnp.where(eye, gcol, 0.0), axis=0, keepdims=True)
        gc_col = jnp.sum(jnp.where(incl, grow, 0.0), axis=1, keepdims=True)
        gc_row = jnp.sum(jnp.where(upper, gcol, 0.0), axis=0, keepdims=True)
        dmat = jnp.exp(jnp.where(incl, gc_col - gc_row, -jnp.inf))
        egc = jnp.exp(gc_col)

        k_b = k_st.astype(BF16)
        kk = _dot_nt(k_b, k_b)
        pw = -jnp.where(strict, bcol * kk * dmat, 0.0)
        pw_b = pw.astype(BF16)
        yinv = pw
        for _ in range(n_squarings):
            pw = _dot(pw_b, pw_b)
            pw_b = pw.astype(BF16)
            yinv = yinv + pw + _dot(yinv, pw_b)
        yinv_b = yinv.astype(BF16)

        rhs_v = bcol * v_st
        rhs_k = (bcol * egc) * k_st
        uv = rhs_v + _dot(yinv_b, rhs_v)
        wk = rhs_k + _dot(yinv_b, rhs_k)
        aqk = _dot_nt(q_st, k_b) * dmat

        us, qss, s_olds = [], [], []
        for t, h in enumerate(heads):
            r0, r1 = t * c, (t + 1) * c
            s_old = s_sc[h]
            res = _dot(jnp.concatenate([wk[r0:r1], q_st[r0:r1]], axis=0), s_old)
            us.append(uv[r0:r1] - res[0:c])
            qss.append(res[c:2 * c])
            s_olds.append(s_old)
        au = _dot(aqk, jnp.concatenate(us, axis=0))
        for t, h in enumerate(heads):
            r0, r1 = t * c, (t + 1) * c
            o = egc[r0:r1] * qss[t] + au[r0:r1]
            gc_last = gc_col[r1 - 1:r1]
            kdec = k_st[r0:r1] * jnp.exp(gc_last - gc_col[r0:r1])
            s_sc[h] = jnp.exp(gc_last) * s_olds[t] + _dot(kdec.T, us[t])
            on = o * lax.rsqrt(jnp.mean(o * o, axis=-1, keepdims=True) + NORM_EPS) * gng_ref[...]
            z = zb_ref[0, :, h * GDN_DV:(h + 1) * GDN_DV].astype(F32)
            yb_ref[0, :, h * GDN_DV:(h + 1) * GDN_DV] = (on * _silu(z)).astype(yb_ref.dtype)

    xp_sc[8 - tail:8, :] = xp_sc[8 + c - tail:8 + c, :]

    @pl.when(i == pl.num_programs(1) - 1)
    def _():
        sout_ref[0] = s_sc[...]


def _gdn(qkv3, qkv_blk, zb3, zb_blk, ba3, conv0, s0, conv_w, a_log, dt_bias, gng, valid_len):
    b, length = ba3.shape[0], ba3.shape[1]
    c = GDN_CHUNK
    n = length // c
    tail = CONV_WIDTH - 1
    const = lambda *shape: pl.BlockSpec(shape, lambda bi, i: (0,) * len(shape))
    return pl.pallas_call(
        functools.partial(_gdn_kernel, valid_len=valid_len),
        grid=(b, n),
        in_specs=[
            pl.BlockSpec((1, c, GDN_CONV_DIM), lambda bi, i: (bi, i, qkv_blk)),
            pl.BlockSpec((1, c, GDN_V), lambda bi, i: (bi, i, zb_blk)),
            pl.BlockSpec((1, c, 128), lambda bi, i: (bi, i, 0)),
            pl.BlockSpec((1, tail, GDN_CONV_DIM), lambda bi, i: (bi, 0, 0)),
            pl.BlockSpec((1, GDN_HEADS, GDN_DK, GDN_DV), lambda bi, i: (bi, 0, 0, 0)),
            const(CONV_WIDTH, GDN_CONV_DIM), const(1, GDN_HEADS), const(1, GDN_HEADS), const(1, GDN_DV),
        ],
        out_specs=[
            pl.BlockSpec((1, c, GDN_V), lambda bi, i: (bi, i, 0)),
            pl.BlockSpec((1, GDN_HEADS, GDN_DK, GDN_DV), lambda bi, i: (bi, 0, 0, 0)),
            pl.BlockSpec((1, tail, GDN_CONV_DIM), lambda bi, i: (bi, 0, 0)),
        ],
        out_shape=[
            jax.ShapeDtypeStruct((b, length, GDN_V), BF16),
            jax.ShapeDtypeStruct((b, GDN_HEADS, GDN_DK, GDN_DV), F32),
            jax.ShapeDtypeStruct((b, tail, GDN_CONV_DIM), F32),
        ],
        scratch_shapes=[pltpu.VMEM((8 + c, GDN_CONV_DIM), F32),
                        pltpu.VMEM((GDN_HEADS, GDN_DK, GDN_DV), F32)],
        compiler_params=_cparams(("parallel", "arbitrary")),
        name="gdn",
    )(qkv3, zb3, ba3, conv0, s0, conv_w.astype(F32), a_log.reshape(1, -1).astype(F32),
      dt_bias.reshape(1, -1).astype(F32), gng.reshape(1, -1).astype(F32))


def _out_kernel(x_ref, oa_ref, za_ref, yb_ref, ym_ref, ga_ref, gb_ref, gm_ref,
                wda_ref, wdb_ref, wdm_ref, wout_ref, fg_ref, o_ref):
    sig = lambda r: _sigmoid(r[...].astype(F32))
    ya = oa_ref[...].astype(F32) * _silu(za_ref[...].astype(F32))
    merged = sig(ga_ref) * _dot(ya, wda_ref[...])
    merged = merged + sig(gb_ref) * _dot(yb_ref[...], wdb_ref[...])
    merged = merged + sig(gm_ref) * _dot(ym_ref[...], wdm_ref[...])
    r = x_ref[...] + _dot(merged, wout_ref[...])
    y = r * lax.rsqrt(jnp.mean(r * r, axis=-1, keepdims=True) + NORM_EPS)
    o_ref[...] = y * fg_ref[...]


def _out_proj(x2d, oa, yb, ym, rest, wda, wdb, wdm, wout, final_g, tm):
    m, d = x2d.shape
    tm = min(tm, m)
    row = lambda width, blk: pl.BlockSpec((tm, width), lambda i: (i, blk))
    resident = lambda shape: pl.BlockSpec(shape, lambda i: (0, 0), pipeline_mode=pl.Buffered(1))
    return pl.pallas_call(
        _out_kernel,
        grid=(m // tm,),
        in_specs=[row(d, 0), row(d, 0), row(d, R_ZA // d), row(d, 0), row(MEM_W, 0),
                  row(d, R_GA // d), row(d, R_GB // d), row(d, R_GM // d),
                  resident(wda.shape), resident(wdb.shape), resident(wdm.shape), resident(wout.shape),
                  resident((1, d))],
        out_specs=row(d, 0),
        out_shape=jax.ShapeDtypeStruct((m, d), F32),
        compiler_params=_cparams(("parallel",)),
        name="out_proj",
    )(x2d, oa, rest, yb, ym, rest, rest, rest, wda, wdb, wdm, wout, final_g.reshape(1, d).astype(F32))


def _project(h, w_q, w_kba, w_v, w_rest, tm):
    scale = DA_HEAD_DIM ** -0.5 * math.log2(math.e)
    (q,) = _matmul(h, w_q, [BF16], tm, 1024, scale=scale, name="proj_q")
    k_f32, k, ba = _proj_k(h, w_kba, min(tm, 512))
    v_f32, v = _matmul(h, w_v, [F32, BF16], tm, 1024, name="proj_v")
    (rest,) = _matmul(h, w_rest, [BF16], tm, 1024, name="proj_rest")
    return q, k_f32, k, v_f32, v, rest, ba


def kernel(x_prompt, x_sample, cache_k, cache_v, cache_mem_k, cache_mem_v, state_delta, state_conv, page_table, mem_prompt, norm_g, w_in, lambda_q1, lambda_k1, lambda_q2, lambda_k2, subln_g, conv_w, a_log, dt_bias, gdn_norm_g, w_down_a, w_down_b, w_down_m, w_out, mem_norm_g, w_mem_kv, final_norm_g):
    assert w_in.shape[0] == 1, "single-layer trunk"
    batch, seq, d = x_prompt.shape
    bd, n_new, _ = x_sample.shape
    assert d == D_MODEL and 2 * n_new == SAMPLE_ROWS and n_new >= CONV_WIDTH - 1
    assert seq % 512 == 0 and seq % GDN_CHUNK == 0

    w = w_in[0]
    o_qa, o_ka, o_za, o_qb, o_zb, o_b, o_qm, o_ga = 0, 2048, 6144, 8192, 12288, 14336, 14352, 15376
    o_va = o_ka + DA_HEADS * DA_HEAD_W
    w_q = w[:, o_qa:o_ka].astype(BF16)
    w_kba = jnp.concatenate([w[:, o_ka:o_va], w[:, o_b:o_qm],
                             jnp.zeros((d, LANES - 2 * GDN_HEADS), w.dtype)], axis=1).astype(BF16)
    w_v = w[:, o_va:o_za].astype(BF16)
    w_rest = jnp.concatenate([w[:, o_qb:o_zb], w[:, o_za:o_qb], w[:, o_zb:o_b], w[:, o_ga:], w[:, o_qm:o_ga]],
                             axis=1).astype(BF16)
    wda, wdb, wdm, wout = (t[0].astype(BF16) for t in (w_down_a, w_down_b, w_down_m, w_out))
    lam_vecs = [t[0].reshape(1, DA_HEAD_DIM).astype(F32) for t in (lambda_q1, lambda_k1, lambda_q2, lambda_k2)]
    sg = subln_g[0].reshape(1, DA_HEAD_W).astype(F32)

    xp2 = x_prompt.reshape(batch * seq, d)
    hp = _rmsnorm(xp2, norm_g[0], BF16, 512)
    q_p, k_p_f32, k_p, v_p_f32, v_p, rest_p, ba_p = _project(hp, w_q, w_kba, w_v, w_rest, 1024)
    oa_p = _attn_prompt(q_p, k_p, v_p, sg, lam_vecs, batch, seq, 512)

    hm = _rmsnorm(mem_prompt.reshape(batch * MEM_LEN, d), mem_norm_g[0], BF16, 512)
    (mem_kv,) = _matmul(hm, w_mem_kv[0].astype(BF16), [F32], 512, 1024, name="proj_mem_kv")
    mk_p = mem_kv[:, :MEM_W].reshape(batch, MEM_LEN, MEM_W)
    mv_p = mem_kv[:, MEM_W:].reshape(batch, MEM_LEN, MEM_W)
    rest_p3 = rest_p.reshape(batch, seq, R_WIDTH)
    ym_p = _mem_attn(rest_p3, mk_p, mv_p, 512)

    conv0 = jnp.zeros((batch, CONV_WIDTH - 1, GDN_CONV_DIM), F32)
    s0 = jnp.zeros((batch, GDN_HEADS, GDN_DK, GDN_DV), F32)
    yb_p, d_p, c_p = _gdn(rest_p3, 0, rest_p3, R_ZB // GDN_V, ba_p.reshape(batch, seq, 128), conv0, s0,
                          conv_w[0], a_log[0], dt_bias[0], gdn_norm_g[0], GDN_CHUNK)

    y_p = _out_proj(xp2, oa_p, yb_p.reshape(batch * seq, GDN_V), ym_p.reshape(batch * seq, MEM_W), rest_p,
                    wda, wdb, wdm, wout, final_norm_g, 256)

    rows = SAMPLE_ROWS
    xs2 = jnp.pad(x_sample, ((0, 0), (0, rows - n_new), (0, 0))).reshape(bd * rows, d)
    hs = _rmsnorm(xs2, norm_g[0], BF16, bd * rows)
    q_s, k_s_f32, _, v_s_f32, _, rest_s, ba_s = _project(hs, w_q, w_kba, w_v, w_rest, bd * rows)
    k_s3 = k_s_f32.reshape(bd, rows, DA_HEADS * DA_HEAD_W)
    v_s3 = v_s_f32.reshape(bd, rows, DA_HEADS * DA_HEAD_W)
    rest_s3 = rest_s.reshape(bd, rows, R_WIDTH)

    q2 = q_s.reshape(bd, rows, DA_HEADS, 2, DA_HEAD_DIM)[:, :n_new]
    q2 = jnp.transpose(q2, (0, 3, 1, 2, 4)).reshape(bd, 2, n_new * DA_HEADS, DA_HEAD_DIM)
    pad_new = ((0, 0), (0, NEW_POSITIONS - rows), (0, 0))
    k_new = jnp.pad(k_s3, pad_new).reshape(bd, NEW_POSITIONS * DA_HEADS * 2, DA_HEAD_DIM)
    v_new = jnp.pad(v_s3, pad_new).reshape(bd, NEW_POSITIONS * DA_HEADS, DA_HEAD_W)
    n_phys = cache_k.shape[1]
    ck = cache_k.reshape(n_phys, PAGE_SIZE * DA_HEADS * 2, DA_HEAD_DIM)
    cv = cache_v.reshape(n_phys, PAGE_SIZE * DA_HEADS, DA_HEAD_W)
    oa_s = _attn_paged(q2, k_new, v_new, sg, lam_vecs, ck, cv, page_table, 8)
    oa_s = jnp.pad(oa_s.reshape(bd, n_new, d), ((0, 0), (0, rows - n_new), (0, 0)))

    mk_s = cache_mem_k[0].reshape(bd, MEM_LEN, MEM_W)
    mv_s = cache_mem_v[0].reshape(bd, MEM_LEN, MEM_W)
    ym_s = _mem_attn(rest_s3, mk_s, mv_s, rows)

    pad_c = ((0, 0), (0, GDN_CHUNK - rows), (0, 0))
    qkv_s = jnp.pad(rest_s3[:, :, R_QKV:R_QKV + GDN_CONV_DIM], pad_c)
    zb_s = jnp.pad(rest_s3[:, :, R_ZB:R_ZB + GDN_V], pad_c)
    ba_s3 = jnp.pad(ba_s.reshape(bd, rows, 128), pad_c)
    yb_s, d_s, c_s = _gdn(qkv_s, 0, zb_s, 0, ba_s3, state_conv[0].astype(F32), state_delta[0].astype(F32),
                          conv_w[0], a_log[0], dt_bias[0], gdn_norm_g[0], n_new)
    yb_s2 = yb_s[:, :rows].reshape(bd * rows, GDN_V)

    y_s = _out_proj(xs2, oa_s.reshape(bd * rows, d), yb_s2, ym_s.reshape(bd * rows, MEM_W), rest_s,
                    wda, wdb, wdm, wout, final_norm_g, bd * rows)

    y_prompt = y_p.reshape(batch, seq, d)
    y_sample = y_s.reshape(bd, rows, d)[:, :n_new]
    new_k_prompt = k_p_f32.reshape(1, batch, seq, DA_HEADS, 2, DA_HEAD_DIM)
    new_v_prompt = v_p_f32.reshape(1, batch, seq, DA_HEADS, DA_HEAD_W)
    new_k_sample = k_s3[:, :n_new].reshape(1, bd, n_new, DA_HEADS, 2, DA_HEAD_DIM)
    new_v_sample = v_s3[:, :n_new].reshape(1, bd, n_new, DA_HEADS, DA_HEAD_W)
    new_mem_k = mk_p.reshape(1, batch, MEM_LEN, MEM_HEADS, MEM_HEAD_DIM)
    new_mem_v = mv_p.reshape(1, batch, MEM_LEN, MEM_HEADS, MEM_HEAD_DIM)
    return (y_prompt, y_sample, new_k_prompt, new_v_prompt, new_k_sample, new_v_sample,
            new_mem_k, new_mem_v, d_p[None], d_s[None], c_p[None], c_s[None])
```

```python
import functools
import math

import jax
import jax.numpy as jnp
import numpy as np
from jax import lax
from jax.experimental import pallas as pl
from jax.experimental.pallas import tpu as pltpu

F32 = jnp.float32
BF16 = jnp.bfloat16

LANES = 128
D_MODEL = 2048
PAGE_SIZE = 128
DA_HEAD_DIM = 128
DA_HEADS = 8
DA_HEAD_W = 2 * DA_HEAD_DIM
GDN_HEADS = 8
GDN_DK = 128
GDN_DV = 256
GDN_QK = GDN_HEADS * GDN_DK
GDN_V = GDN_HEADS * GDN_DV
GDN_CONV_DIM = 2 * GDN_QK + GDN_V
CONV_WIDTH = 4
GDN_CHUNK = 64
GDN_GROUP = 4
MEM_LEN = 256
MEM_HEADS = 4
MEM_HEAD_DIM = 128
MEM_W = MEM_HEADS * MEM_HEAD_DIM
NORM_EPS = 1e-6
SUBLN_EPS = 1e-5
LAMBDA_INIT = 0.8 - 0.6 * math.exp(-0.3 * 0)
NEW_POSITIONS = LANES // DA_HEADS
SAMPLE_ROWS = 8

R_QKV = 0
R_ZA = 4096
R_ZB = 6144
R_GA = 8192
R_GB = 10240
R_GM = 12288
R_QM = 14336
R_ZM = 14848
R_WIDTH = 15360

VMEM_LIMIT = 56 * 1024 * 1024


def _cparams(sem):
    return pltpu.CompilerParams(dimension_semantics=sem, vmem_limit_bytes=VMEM_LIMIT)


def _sigmoid(x):
    return 0.5 * jnp.tanh(0.5 * x) + 0.5


def _silu(x):
    return x * _sigmoid(x)


def _dot(a, b):
    return jnp.dot(a.astype(BF16), b.astype(BF16), preferred_element_type=F32)


def _dot_nt(a, b):
    return lax.dot_general(a.astype(BF16), b.astype(BF16), (((1,), (1,)), ((), ())),
                           preferred_element_type=F32)


def _rmsnorm_kernel(x_ref, g_ref, o_ref, *, eps):
    x = x_ref[...].astype(F32)
    y = x * lax.rsqrt(jnp.mean(x * x, axis=-1, keepdims=True) + eps)
    o_ref[...] = (y * g_ref[...]).astype(o_ref.dtype)


def _rmsnorm(x2d, g, out_dtype, tm):
    m, d = x2d.shape
    tm = min(tm, m)
    return pl.pallas_call(
        functools.partial(_rmsnorm_kernel, eps=NORM_EPS),
        grid=(m // tm,),
        in_specs=[pl.BlockSpec((tm, d), lambda i: (i, 0)),
                  pl.BlockSpec((1, d), lambda i: (0, 0))],
        out_specs=pl.BlockSpec((tm, d), lambda i: (i, 0)),
        out_shape=jax.ShapeDtypeStruct((m, d), out_dtype),
        compiler_params=_cparams(("parallel",)),
        name="rmsnorm",
    )(x2d, g.reshape(1, d).astype(F32))


def _matmul_kernel(h_ref, w_ref, *o_refs, scale):
    acc = jnp.dot(h_ref[...], w_ref[...], preferred_element_type=F32)
    if scale is not None:
        acc = acc * scale
    for o_ref in o_refs:
        o_ref[...] = acc.astype(o_ref.dtype)


def _matmul(h, w, out_dtypes, tm, tn, scale=None, name="matmul"):
    m, k = h.shape
    n = w.shape[1]
    tm, tn = min(tm, m), min(tn, n)
    outs = pl.pallas_call(
        functools.partial(_matmul_kernel, scale=scale),
        grid=(m // tm, n // tn),
        in_specs=[pl.BlockSpec((tm, k), lambda i, j: (i, 0)),
                  pl.BlockSpec((k, tn), lambda i, j: (0, j))],
        out_specs=[pl.BlockSpec((tm, tn), lambda i, j: (i, j)) for _ in out_dtypes],
        out_shape=[jax.ShapeDtypeStruct((m, n), dt) for dt in out_dtypes],
        compiler_params=_cparams(("parallel", "parallel")),
        name=name,
    )(h, w)
    return outs


def _proj_k_kernel(h_ref, w_ref, k_ref, kb_ref, ba_ref):
    acc = jnp.dot(h_ref[...], w_ref[...], preferred_element_type=F32)
    tm, n = kb_ref.shape
    parts = n // DA_HEAD_DIM
    for j in range(parts):
        k_ref[pl.ds(j, tm, stride=parts), :] = acc[:, j * DA_HEAD_DIM:(j + 1) * DA_HEAD_DIM]
    kb_ref[...] = acc[:, :n].astype(kb_ref.dtype)
    ba_ref[...] = acc[:, n:]


def _proj_k(h, w_kba, tm):
    m, k = h.shape
    n = w_kba.shape[1] - LANES
    parts = n // DA_HEAD_DIM
    tm = min(tm, m)
    return pl.pallas_call(
        _proj_k_kernel,
        grid=(m // tm,),
        in_specs=[pl.BlockSpec((tm, k), lambda i: (i, 0)),
                  pl.BlockSpec(w_kba.shape, lambda i: (0, 0), pipeline_mode=pl.Buffered(1))],
        out_specs=[pl.BlockSpec((tm * parts, DA_HEAD_DIM), lambda i: (i, 0)),
                   pl.BlockSpec((tm, n), lambda i: (i, 0)),
                   pl.BlockSpec((tm, LANES), lambda i: (i, 0))],
        out_shape=[jax.ShapeDtypeStruct((m * parts, DA_HEAD_DIM), F32), jax.ShapeDtypeStruct((m, n), BF16),
                   jax.ShapeDtypeStruct((m, LANES), F32)],
        compiler_params=_cparams(("parallel",)),
        name="proj_k",
    )(h, w_kba)


def _diff_lambda(lq1_ref, lk1_ref, lq2_ref, lk2_ref):
    s1 = jnp.sum(lq1_ref[...] * lk1_ref[...], axis=-1, keepdims=True)
    s2 = jnp.sum(lq2_ref[...] * lk2_ref[...], axis=-1, keepdims=True)
    return jnp.exp(s1) - jnp.exp(s2) + LAMBDA_INIT


def _subln(od, g):
    y = od * lax.rsqrt(jnp.mean(od * od, axis=-1, keepdims=True) + SUBLN_EPS)
    return (y * g) * (1.0 - LAMBDA_INIT)


def _attn_prompt_kernel(qi_tab, ki_tab, q_ref, k_ref, v_ref, g_ref,
                        lq1_ref, lk1_ref, lq2_ref, lk2_ref, o_ref, m_sc, l_sc, acc_sc):
    step = pl.program_id(2)
    qi = qi_tab[step]
    ki = ki_tab[step]
    tq = q_ref.shape[0]
    lane_tiles = tq // LANES

    @pl.when(ki == 0)
    def _():
        m_sc[...] = jnp.full(m_sc.shape, -jnp.inf, F32)
        l_sc[...] = jnp.zeros(l_sc.shape, F32)
        acc_sc[...] = jnp.zeros(acc_sc.shape, F32)

    heads = q_ref.shape[1] // DA_HEAD_W
    reps = DA_HEAD_W // LANES

    def update(masked):
        if masked:
            row = lax.broadcasted_iota(jnp.int32, (tq, tq), 0)
            col = lax.broadcasted_iota(jnp.int32, (tq, tq), 1)
            keep = col <= row
        for hh in range(heads):
            v = v_ref[:, hh * DA_HEAD_W:(hh + 1) * DA_HEAD_W]
            for c in range(2):
                lo = hh * DA_HEAD_W + c * DA_HEAD_DIM
                s = lax.dot_general(q_ref[:, lo:lo + DA_HEAD_DIM], k_ref[:, lo:lo + DA_HEAD_DIM],
                                    (((1,), (1,)), ((), ())), preferred_element_type=F32)
                if masked:
                    s = jnp.where(keep, s, -jnp.inf)
                n = 2 * hh + c
                m_prev = m_sc[n]
                m_new = jnp.maximum(m_prev, jnp.max(s, axis=-1, keepdims=True))
                alpha = jnp.exp2(m_prev - m_new)
                p = jnp.exp2(s - pltpu.repeat(m_new, lane_tiles, axis=1))
                l_sc[n] = alpha * l_sc[n] + jnp.sum(p, axis=-1, keepdims=True)
                acc_sc[n] = (pltpu.repeat(alpha, reps, axis=1) * acc_sc[n]
                             + jnp.dot(p.astype(BF16), v, preferred_element_type=F32))
                m_sc[n] = m_new

    @pl.when(ki < qi)
    def _():
        update(False)

    @pl.when(ki == qi)
    def _():
        update(True)
        lam = _diff_lambda(lq1_ref, lk1_ref, lq2_ref, lk2_ref)
        for hh in range(heads):
            od = (acc_sc[2 * hh] / pltpu.repeat(l_sc[2 * hh], reps, axis=1)
                  - lam * (acc_sc[2 * hh + 1] / pltpu.repeat(l_sc[2 * hh + 1], reps, axis=1)))
            o_ref[:, hh * DA_HEAD_W:(hh + 1) * DA_HEAD_W] = _subln(od, g_ref[...]).astype(o_ref.dtype)


def _attn_prompt(q, k, v, subln_g, lam_vecs, batch, seq, tq, heads_per_step):
    m = batch * seq
    nq = seq // tq
    hw = heads_per_step * DA_HEAD_W
    qi_tab = np.concatenate([np.full(i + 1, i, np.int32) for i in range(nq)])
    ki_tab = np.concatenate([np.arange(i + 1, dtype=np.int32) for i in range(nq)])
    n_steps = int(qi_tab.shape[0])
    vec = pl.BlockSpec((1, DA_HEAD_DIM), lambda b, h, s, qt, kt: (0, 0))
    grid_spec = pltpu.PrefetchScalarGridSpec(
        num_scalar_prefetch=2,
        grid=(batch, DA_HEADS // heads_per_step, n_steps),
        in_specs=[
            pl.BlockSpec((tq, hw), lambda b, h, s, qt, kt: (b * nq + qt[s], h)),
            pl.BlockSpec((tq, hw), lambda b, h, s, qt, kt: (b * nq + kt[s], h)),
            pl.BlockSpec((tq, hw), lambda b, h, s, qt, kt: (b * nq + kt[s], h)),
            pl.BlockSpec((1, DA_HEAD_W), lambda b, h, s, qt, kt: (0, 0)),
            vec, vec, vec, vec,
        ],
        out_specs=pl.BlockSpec((tq, hw), lambda b, h, s, qt, kt: (b * nq + qt[s], h)),
        scratch_shapes=[pltpu.VMEM((2 * heads_per_step, tq, LANES), F32),
                        pltpu.VMEM((2 * heads_per_step, tq, LANES), F32),
                        pltpu.VMEM((2 * heads_per_step, tq, DA_HEAD_W), F32)],
    )
    return pl.pallas_call(
        _attn_prompt_kernel,
        grid_spec=grid_spec,
        out_shape=jax.ShapeDtypeStruct((m, DA_HEADS * DA_HEAD_W), BF16),
        compiler_params=_cparams(("parallel", "parallel", "arbitrary")),
        name="attn_prompt",
    )(jnp.asarray(qi_tab), jnp.asarray(ki_tab), q, k, v, subln_g, *lam_vecs)


def _attn_paged_kernel(pt_ref, q_ref, kn_ref, vn_ref, g_ref, lq1_ref, lk1_ref, lq2_ref, lk2_ref,
                       *refs, pages_per_step):
    k_refs = refs[:pages_per_step]
    v_refs = refs[pages_per_step:2 * pages_per_step]
    o_ref = refs[2 * pages_per_step]
    m_sc, l_sc, acc_sc = refs[2 * pages_per_step + 1:]
    j = pl.program_id(1)
    nq = q_ref.shape[2]
    head_bits = DA_HEADS.bit_length() - 1

    def same_head(ncols):
        row = lax.broadcasted_iota(jnp.int32, (nq, ncols), 0)
        col = lax.broadcasted_iota(jnp.int32, (nq, ncols), 1)
        return row, col, (row & (DA_HEADS - 1)) == (col & (DA_HEADS - 1))

    def update(kv_refs, keep):
        ps, alphas = [], []
        for c in range(2):
            ss = []
            for k_ref, v_ref in kv_refs:
                k_c = k_ref[0, pl.ds(c, v_ref.shape[1], stride=2), :].astype(BF16)
                s = lax.dot_general(q_ref[0, c], k_c, (((1,), (1,)), ((), ())), preferred_element_type=F32)
                ss.append(jnp.where(keep, s, -jnp.inf))
            rows = pl.ds(c * nq, nq)
            m_prev = m_sc[rows, :]
            m_new = m_prev
            for s in ss:
                m_new = jnp.maximum(m_new, jnp.max(s, axis=-1, keepdims=True))
            alpha = jnp.exp2(m_prev - m_new)
            p_c = [jnp.exp2(s - m_new) for s in ss]
            l_new = alpha * l_sc[rows, :]
            for p in p_c:
                l_new = l_new + jnp.sum(p, axis=-1, keepdims=True)
            l_sc[rows, :] = l_new
            m_sc[rows, :] = m_new
            ps.append([p.astype(BF16) for p in p_c])
            alphas.append(alpha)
        acc = jnp.concatenate(alphas, axis=0) * acc_sc[...]
        for r, (_, v_ref) in enumerate(kv_refs):
            acc = acc + jnp.dot(jnp.concatenate([ps[0][r], ps[1][r]], axis=0), v_ref[0].astype(BF16),
                                preferred_element_type=F32)
        acc_sc[...] = acc

    @pl.when(j == 0)
    def _():
        m_sc[...] = jnp.full(m_sc.shape, -jnp.inf, F32)
        l_sc[...] = jnp.zeros(l_sc.shape, F32)
        acc_sc[...] = jnp.zeros(acc_sc.shape, F32)
        row, col, keep = same_head(vn_ref.shape[1])
        keep = keep & ((col >> head_bits) <= (row >> head_bits))
        update([(kn_ref, vn_ref)], keep)

    _, _, keep_page = same_head(PAGE_SIZE * DA_HEADS)
    update(list(zip(k_refs, v_refs)), keep_page)

    @pl.when(j == pl.num_programs(1) - 1)
    def _():
        lam = _diff_lambda(lq1_ref, lk1_ref, lq2_ref, lk2_ref)
        o = acc_sc[...] / l_sc[...]
        od = o[0:nq] - lam * o[nq:2 * nq]
        o_ref[0] = _subln(od, g_ref[...])


def _attn_paged(q2, k_new, v_new, subln_g, lam_vecs, cache_k, cache_v, page_table, pages_per_step):
    bd, n_pages = page_table.shape
    n_steps = n_pages // pages_per_step
    nq = q2.shape[2]
    pt_flat = page_table.reshape(-1)
    page = lambda b, j, pt, r: (pt[b * n_pages + j * pages_per_step + r], 0, 0)
    k_specs = [pl.BlockSpec((1,) + cache_k.shape[1:], functools.partial(page, r=r)) for r in range(pages_per_step)]
    v_specs = [pl.BlockSpec((1,) + cache_v.shape[1:], functools.partial(page, r=r)) for r in range(pages_per_step)]
    vec = pl.BlockSpec((1, DA_HEAD_DIM), lambda b, j, pt: (0, 0))
    per_seq = lambda a: pl.BlockSpec((1,) + a.shape[1:], lambda b, j, pt: (b,) + (0,) * (a.ndim - 1))
    grid_spec = pltpu.PrefetchScalarGridSpec(
        num_scalar_prefetch=1,
        grid=(bd, n_steps),
        in_specs=[per_seq(q2), per_seq(k_new), per_seq(v_new),
                  pl.BlockSpec((1, DA_HEAD_W), lambda b, j, pt: (0, 0)),
                  vec, vec, vec, vec] + k_specs + v_specs,
        out_specs=pl.BlockSpec((1, nq, DA_HEAD_W), lambda b, j, pt: (b, 0, 0)),
        scratch_shapes=[pltpu.VMEM((2 * nq, 1), F32), pltpu.VMEM((2 * nq, 1), F32),
                        pltpu.VMEM((2 * nq, DA_HEAD_W), F32)],
    )
    return pl.pallas_call(
        functools.partial(_attn_paged_kernel, pages_per_step=pages_per_step),
        grid_spec=grid_spec,
        out_shape=jax.ShapeDtypeStruct((bd, nq, DA_HEAD_W), F32),
        compiler_params=_cparams(("parallel", "arbitrary")),
        name="attn_paged",
    )(pt_flat, q2, k_new, v_new, subln_g, *lam_vecs,
      *([cache_k] * pages_per_step), *([cache_v] * pages_per_step))


def _mem_attn_kernel(q_ref, z_ref, mk_ref, mv_ref, o_ref):
    scale = MEM_HEAD_DIM ** -0.5
    for h in range(MEM_HEADS):
        lo, hi = h * MEM_HEAD_DIM, (h + 1) * MEM_HEAD_DIM
        s = _dot_nt(q_ref[0, :, lo:hi], mk_ref[0, :, lo:hi]) * scale
        s = s - jnp.max(s, axis=-1, keepdims=True)
        e = jnp.exp(s)
        p = e / jnp.sum(e, axis=-1, keepdims=True)
        o = _dot(p, mv_ref[0, :, lo:hi])
        o_ref[0, :, lo:hi] = (o * _silu(z_ref[0, :, lo:hi].astype(F32))).astype(o_ref.dtype)


def _mem_attn(rest3, mk, mv, tm):
    b, length, _ = rest3.shape
    tm = min(tm, length)
    return pl.pallas_call(
        _mem_attn_kernel,
        grid=(b, length // tm),
        in_specs=[pl.BlockSpec((1, tm, MEM_W), lambda i, j: (i, j, R_QM // MEM_W)),
                  pl.BlockSpec((1, tm, MEM_W), lambda i, j: (i, j, R_ZM // MEM_W)),
                  pl.BlockSpec((1, MEM_LEN, MEM_W), lambda i, j: (i, 0, 0)),
                  pl.BlockSpec((1, MEM_LEN, MEM_W), lambda i, j: (i, 0, 0))],
        out_specs=pl.BlockSpec((1, tm, MEM_W), lambda i, j: (i, j, 0)),
        out_shape=jax.ShapeDtypeStruct((b, length, MEM_W), BF16),
        compiler_params=_cparams(("parallel", "parallel")),
        name="mem_attn",
    )(rest3, rest3, mk, mv)


def _softplus(x):
    return jnp.maximum(x, 0.0) + jnp.log1p(jnp.exp(-jnp.abs(x)))


def _l2n(x):
    return x * lax.rsqrt(jnp.sum(x * x, axis=-1, keepdims=True) + 1e-6)


def _gdn_kernel(qkv_ref, zb_ref, ba_ref, conv0_ref, s0_ref, convw_ref, alog_ref, dtb_ref, gng_ref,
                yb_ref, sout_ref, convout_ref, xp_sc, s_sc, *, valid_len):
    i = pl.program_id(1)
    c = GDN_CHUNK
    gw = GDN_GROUP * c
    tail = CONV_WIDTH - 1

    @pl.when(i == 0)
    def _():
        xp_sc[8 - tail:8, :] = conv0_ref[0]
        s_sc[...] = s0_ref[0]

    xp_sc[8:8 + c, :] = qkv_ref[0].astype(F32)

    def conv_silu(lo, width):
        y = xp_sc[8 - tail:8 - tail + c, lo:lo + width] * convw_ref[0:1, lo:lo + width]
        for t in range(1, CONV_WIDTH):
            y = y + xp_sc[8 - tail + t:8 - tail + t + c, lo:lo + width] * convw_ref[t:t + 1, lo:lo + width]
        return _silu(y)

    @pl.when(i == pl.num_programs(1) - 1)
    def _():
        convout_ref[0] = xp_sc[8 + valid_len - tail:8 + valid_len, :]

    ba = ba_ref[0]
    if valid_len < c:
        live = lax.broadcasted_iota(jnp.int32, (c, 1), 0) < valid_len

    ii = lax.broadcasted_iota(jnp.int32, (gw, gw), 0)
    jj = lax.broadcasted_iota(jnp.int32, (gw, gw), 1)
    shift = int(math.log2(c))
    same = lax.shift_right_logical(ii, shift) == lax.shift_right_logical(jj, shift)
    incl = same & (jj <= ii)
    strict = same & (jj < ii)
    upper = same & (ii <= jj)
    eye = ii == jj

    n_squarings = max(math.ceil(math.log2(valid_len)) - 1, 0)

    for grp in range(GDN_HEADS // GDN_GROUP):
        heads = [grp * GDN_GROUP + t for t in range(GDN_GROUP)]
        q_g = conv_silu(grp * GDN_GROUP * GDN_DK, GDN_GROUP * GDN_DK)
        k_g = conv_silu(GDN_QK + grp * GDN_GROUP * GDN_DK, GDN_GROUP * GDN_DK)
        v_g = conv_silu(2 * GDN_QK + grp * GDN_GROUP * GDN_DV, GDN_GROUP * GDN_DV)
        qs, ks, vs, bcols, gcols = [], [], [], [], []
        for t, h in enumerate(heads):
            qs.append(_l2n(q_g[:, t * GDN_DK:(t + 1) * GDN_DK]) * (GDN_DK ** -0.5))
            ks.append(_l2n(k_g[:, t * GDN_DK:(t + 1) * GDN_DK]))
            vs.append(v_g[:, t * GDN_DV:(t + 1) * GDN_DV])
            beta = _sigmoid(ba[:, h:h + 1])
            g = -jnp.exp(alog_ref[0:1, h:h + 1]) * _softplus(
                ba[:, GDN_HEADS + h:GDN_HEADS + h + 1] + dtb_ref[0:1, h:h + 1])
            if valid_len < c:
                beta = jnp.where(live, beta, 0.0)
                g = jnp.where(live, g, 0.0)
            bcols.append(beta)
            gcols.append(g)
        q_st = jnp.concatenate(qs, axis=0)
        k_st = jnp.concatenate(ks, axis=0)
        v_st = jnp.concatenate(vs, axis=0)
        bcol = jnp.concatenate(bcols, axis=0)
        gcol = jnp.concatenate(gcols, axis=0)

        grow = jnp.sum(jnp.where(eye, gcol, 0.0), axis=0, keepdims=True)
        gc_col = jnp.sum(jnp.where(incl, grow, 0.0), axis=1, keepdims=True)
        gc_row = jnp.sum(jnp.where(upper, gcol, 0.0), axis=0, keepdims=True)
        dmat = jnp.exp(jnp.where(incl, gc_col - gc_row, -jnp.inf))
        egc = jnp.exp(gc_col)

        k_b = k_st.astype(BF16)
        kk = _dot_nt(k_b, k_b)
        pw = -jnp.where(strict, bcol * kk * dmat, 0.0)
        pw_b = pw.astype(BF16)
        yinv = pw
        for _ in range(n_squarings):
            pw = _dot(pw_b, pw_b)
            pw_b = pw.astype(BF16)
            yinv = yinv + pw + _dot(yinv, pw_b)
        yinv_b = yinv.astype(BF16)

        rhs_v = bcol * v_st
        rhs_k = (bcol * egc) * k_st
        uv = rhs_v + _dot(yinv_b, rhs_v)
        wk = rhs_k + _dot(yinv_b, rhs_k)
        aqk = _dot_nt(q_st, k_b) * dmat

        us, qss, s_olds = [], [], []
        for t, h in enumerate(heads):
            r0, r1 = t * c, (t + 1) * c
            s_old = s_sc[h]
            res = _dot(jnp.concatenate([wk[r0:r1], q_st[r0:r1]], axis=0), s_old)
            us.append(uv[r0:r1] - res[0:c])
            qss.append(res[c:2 * c])
            s_olds.append(s_old)
        au = _dot(aqk, jnp.concatenate(us, axis=0))
        for t, h in enumerate(heads):
            r0, r1 = t * c, (t + 1) * c
            o = egc[r0:r1] * qss[t] + au[r0:r1]
            gc_last = gc_col[r1 - 1:r1]
            kdec = k_st[r0:r1] * jnp.exp(gc_last - gc_col[r0:r1])
            s_sc[h] = jnp.exp(gc_last) * s_olds[t] + _dot(kdec.T, us[t])
            on = o * lax.rsqrt(jnp.mean(o * o, axis=-1, keepdims=True) + NORM_EPS) * gng_ref[...]
            z = zb_ref[0, :, h * GDN_DV:(h + 1) * GDN_DV].astype(F32)
            yb_ref[0, :, h * GDN_DV:(h + 1) * GDN_DV] = (on * _silu(z)).astype(yb_ref.dtype)

    xp_sc[8 - tail:8, :] = xp_sc[8 + c - tail:8 + c, :]

    @pl.when(i == pl.num_programs(1) - 1)
    def _():
        sout_ref[0] = s_sc[...]


def _gdn(qkv3, qkv_blk, zb3, zb_blk, ba3, conv0, s0, conv_w, a_log, dt_bias, gng, valid_len):
    b, length = ba3.shape[0], ba3.shape[1]
    c = GDN_CHUNK
    n = length // c
    tail = CONV_WIDTH - 1
    const = lambda *shape: pl.BlockSpec(shape, lambda bi, i: (0,) * len(shape))
    return pl.pallas_call(
        functools.partial(_gdn_kernel, valid_len=valid_len),
        grid=(b, n),
        in_specs=[
            pl.BlockSpec((1, c, GDN_CONV_DIM), lambda bi, i: (bi, i, qkv_blk)),
            pl.BlockSpec((1, c, GDN_V), lambda bi, i: (bi, i, zb_blk)),
            pl.BlockSpec((1, c, 128), lambda bi, i: (bi, i, 0)),
            pl.BlockSpec((1, tail, GDN_CONV_DIM), lambda bi, i: (bi, 0, 0)),
            pl.BlockSpec((1, GDN_HEADS, GDN_DK, GDN_DV), lambda bi, i: (bi, 0, 0, 0)),
            const(CONV_WIDTH, GDN_CONV_DIM), const(1, GDN_HEADS), const(1, GDN_HEADS), const(1, GDN_DV),
        ],
        out_specs=[
            pl.BlockSpec((1, c, GDN_V), lambda bi, i: (bi, i, 0)),
            pl.BlockSpec((1, GDN_HEADS, GDN_DK, GDN_DV), lambda bi, i: (bi, 0, 0, 0)),
            pl.BlockSpec((1, tail, GDN_CONV_DIM), lambda bi, i: (bi, 0, 0)),
        ],
        out_shape=[
            jax.ShapeDtypeStruct((b, length, GDN_V), BF16),
            jax.ShapeDtypeStruct((b, GDN_HEADS, GDN_DK, GDN_DV), F32),
            jax.ShapeDtypeStruct((b, tail, GDN_CONV_DIM), F32),
        ],
        scratch_shapes=[pltpu.VMEM((8 + c, GDN_CONV_DIM), F32),
                        pltpu.VMEM((GDN_HEADS, GDN_DK, GDN_DV), F32)],
        compiler_params=_cparams(("parallel", "arbitrary")),
        name="gdn",
    )(qkv3, zb3, ba3, conv0, s0, conv_w.astype(F32), a_log.reshape(1, -1).astype(F32),
      dt_bias.reshape(1, -1).astype(F32), gng.reshape(1, -1).astype(F32))


def _out_kernel(x_ref, oa_ref, za_ref, yb_ref, ym_ref, ga_ref, gb_ref, gm_ref,
                wda_ref, wdb_ref, wdm_ref, wout_ref, fg_ref, o_ref):
    sig = lambda r: _sigmoid(r[...].astype(F32))
    ya = oa_ref[...].astype(F32) * _silu(za_ref[...].astype(F32))
    merged = sig(ga_ref) * _dot(ya, wda_ref[...])
    merged = merged + sig(gb_ref) * _dot(yb_ref[...], wdb_ref[...])
    merged = merged + sig(gm_ref) * _dot(ym_ref[...], wdm_ref[...])
    r = x_ref[...] + _dot(merged, wout_ref[...])
    y = r * lax.rsqrt(jnp.mean(r * r, axis=-1, keepdims=True) + NORM_EPS)
    o_ref[...] = y * fg_ref[...]


def _out_proj(x2d, oa, yb, ym, rest, wda, wdb, wdm, wout, final_g, tm):
    m, d = x2d.shape
    tm = min(tm, m)
    row = lambda width, blk: pl.BlockSpec((tm, width), lambda i: (i, blk))
    resident = lambda shape: pl.BlockSpec(shape, lambda i: (0, 0), pipeline_mode=pl.Buffered(1))
    return pl.pallas_call(
        _out_kernel,
        grid=(m // tm,),
        in_specs=[row(d, 0), row(d, 0), row(d, R_ZA // d), row(d, 0), row(MEM_W, 0),
                  row(d, R_GA // d), row(d, R_GB // d), row(d, R_GM // d),
                  resident(wda.shape), resident(wdb.shape), resident(wdm.shape), resident(wout.shape),
                  resident((1, d))],
        out_specs=row(d, 0),
        out_shape=jax.ShapeDtypeStruct((m, d), F32),
        compiler_params=_cparams(("parallel",)),
        name="out_proj",
    )(x2d, oa, rest, yb, ym, rest, rest, rest, wda, wdb, wdm, wout, final_g.reshape(1, d).astype(F32))


def _project(h, w_q, w_kba, w_v, w_rest, tm):
    scale = DA_HEAD_DIM ** -0.5 * math.log2(math.e)
    (q,) = _matmul(h, w_q, [BF16], tm, 1024, scale=scale, name="proj_q")
    k_f32, k, ba = _proj_k(h, w_kba, min(tm, 512))
    v_f32, v = _matmul(h, w_v, [F32, BF16], tm, 1024, name="proj_v")
    (rest,) = _matmul(h, w_rest, [BF16], tm, 1024, name="proj_rest")
    return q, k_f32, k, v_f32, v, rest, ba


def kernel(x_prompt, x_sample, cache_k, cache_v, cache_mem_k, cache_mem_v, state_delta, state_conv, page_table, mem_prompt, norm_g, w_in, lambda_q1, lambda_k1, lambda_q2, lambda_k2, subln_g, conv_w, a_log, dt_bias, gdn_norm_g, w_down_a, w_down_b, w_down_m, w_out, mem_norm_g, w_mem_kv, final_norm_g):
    assert w_in.shape[0] == 1, "single-layer trunk"
    batch, seq, d = x_prompt.shape
    bd, n_new, _ = x_sample.shape
    assert d == D_MODEL and 2 * n_new == SAMPLE_ROWS and n_new >= CONV_WIDTH - 1
    assert seq % 512 == 0 and seq % GDN_CHUNK == 0

    w = w_in[0]
    o_qa, o_ka, o_za, o_qb, o_zb, o_b, o_qm, o_ga = 0, 2048, 6144, 8192, 12288, 14336, 14352, 15376
    o_va = o_ka + DA_HEADS * DA_HEAD_W
    w_q = w[:, o_qa:o_ka].astype(BF16)
    w_kba = jnp.concatenate([w[:, o_ka:o_va], w[:, o_b:o_qm],
                             jnp.zeros((d, LANES - 2 * GDN_HEADS), w.dtype)], axis=1).astype(BF16)
    w_v = w[:, o_va:o_za].astype(BF16)
    w_rest = jnp.concatenate([w[:, o_qb:o_zb], w[:, o_za:o_qb], w[:, o_zb:o_b], w[:, o_ga:], w[:, o_qm:o_ga]],
                             axis=1).astype(BF16)
    wda, wdb, wdm, wout = (t[0].astype(BF16) for t in (w_down_a, w_down_b, w_down_m, w_out))
    lam_vecs = [t[0].reshape(1, DA_HEAD_DIM).astype(F32) for t in (lambda_q1, lambda_k1, lambda_q2, lambda_k2)]
    sg = subln_g[0].reshape(1, DA_HEAD_W).astype(F32)

    xp2 = x_prompt.reshape(batch * seq, d)
    hp = _rmsnorm(xp2, norm_g[0], BF16, 512)
    q_p, k_p_f32, k_p, v_p_f32, v_p, rest_p, ba_p = _project(hp, w_q, w_kba, w_v, w_rest, 1024)
    oa_p = _attn_prompt(q_p, k_p, v_p, sg, lam_vecs, batch, seq, 512, 4)

    hm = _rmsnorm(mem_prompt.reshape(batch * MEM_LEN, d), mem_norm_g[0], BF16, 512)
    (mem_kv,) = _matmul(hm, w_mem_kv[0].astype(BF16), [F32], 512, 1024, name="proj_mem_kv")
    mk_p = mem_kv[:, :MEM_W].reshape(batch, MEM_LEN, MEM_W)
    mv_p = mem_kv[:, MEM_W:].reshape(batch, MEM_LEN, MEM_W)
    rest_p3 = rest_p.reshape(batch, seq, R_WIDTH)
    ym_p = _mem_attn(rest_p3, mk_p, mv_p, 512)

    conv0 = jnp.zeros((batch, CONV_WIDTH - 1, GDN_CONV_DIM), F32)
    s0 = jnp.zeros((batch, GDN_HEADS, GDN_DK, GDN_DV), F32)
    yb_p, d_p, c_p = _gdn(rest_p3, 0, rest_p3, R_ZB // GDN_V, ba_p.reshape(batch, seq, 128), conv0, s0,
                          conv_w[0], a_log[0], dt_bias[0], gdn_norm_g[0], GDN_CHUNK)

    y_p = _out_proj(xp2, oa_p, yb_p.reshape(batch * seq, GDN_V), ym_p.reshape(batch * seq, MEM_W), rest_p,
                    wda, wdb, wdm, wout, final_norm_g, 256)

    rows = SAMPLE_ROWS
    xs2 = jnp.pad(x_sample, ((0, 0), (0, rows - n_new), (0, 0))).reshape(bd * rows, d)
    hs = _rmsnorm(xs2, norm_g[0], BF16, bd * rows)
    q_s, k_s_f32, _, v_s_f32, _, rest_s, ba_s = _project(hs, w_q, w_kba, w_v, w_rest, bd * rows)
    k_s3 = k_s_f32.reshape(bd, rows, DA_HEADS * DA_HEAD_W)
    v_s3 = v_s_f32.reshape(bd, rows, DA_HEADS * DA_HEAD_W)
    rest_s3 = rest_s.reshape(bd, rows, R_WIDTH)

    q2 = q_s.reshape(bd, rows, DA_HEADS, 2, DA_HEAD_DIM)[:, :n_new]
    q2 = jnp.transpose(q2, (0, 3, 1, 2, 4)).reshape(bd, 2, n_new * DA_HEADS, DA_HEAD_DIM)
    pad_new = ((0, 0), (0, NEW_POSITIONS - rows), (0, 0))
    k_new = jnp.pad(k_s3, pad_new).reshape(bd, NEW_POSITIONS * DA_HEADS * 2, DA_HEAD_DIM)
    v_new = jnp.pad(v_s3, pad_new).reshape(bd, NEW_POSITIONS * DA_HEADS, DA_HEAD_W)
    n_phys = cache_k.shape[1]
    ck = cache_k.reshape(n_phys, PAGE_SIZE * DA_HEADS * 2, DA_HEAD_DIM)
    cv = cache_v.reshape(n_phys, PAGE_SIZE * DA_HEADS, DA_HEAD_W)
    oa_s = _attn_paged(q2, k_new, v_new, sg, lam_vecs, ck, cv, page_table, 8)
    oa_s = jnp.pad(oa_s.reshape(bd, n_new, d), ((0, 0), (0, rows - n_new), (0, 0)))

    mk_s = cache_mem_k[0].reshape(bd, MEM_LEN, MEM_W)
    mv_s = cache_mem_v[0].reshape(bd, MEM_LEN, MEM_W)
    ym_s = _mem_attn(rest_s3, mk_s, mv_s, rows)

    pad_c = ((0, 0), (0, GDN_CHUNK - rows), (0, 0))
    qkv_s = jnp.pad(rest_s3[:, :, R_QKV:R_QKV + GDN_CONV_DIM], pad_c)
    zb_s = jnp.pad(rest_s3[:, :, R_ZB:R_ZB + GDN_V], pad_c)
    ba_s3 = jnp.pad(ba_s.reshape(bd, rows, 128), pad_c)
    yb_s, d_s, c_s = _gdn(qkv_s, 0, zb_s, 0, ba_s3, state_conv[0].astype(F32), state_delta[0].astype(F32),
                          conv_w[0], a_log[0], dt_bias[0], gdn_norm_g[0], n_new)
    yb_s2 = yb_s[:, :rows].reshape(bd * rows, GDN_V)

    y_s = _out_proj(xs2, oa_s.reshape(bd * rows, d), yb_s2, ym_s.reshape(bd * rows, MEM_W), rest_s,
                    wda, wdb, wdm, wout, final_norm_g, bd * rows)

    y_prompt = y_p.reshape(batch, seq, d)
    y_sample = y_s.reshape(bd, rows, d)[:, :n_new]
    new_k_prompt = k_p_f32.reshape(1, batch, seq, DA_HEADS, 2, DA_HEAD_DIM)
    new_v_prompt = v_p_f32.reshape(1, batch, seq, DA_HEADS, DA_HEAD_W)
    new_k_sample = k_s3[:, :n_new].reshape(1, bd, n_new, DA_HEADS, 2, DA_HEAD_DIM)
    new_v_sample = v_s3[:, :n_new].reshape(1, bd, n_new, DA_HEADS, DA_HEAD_W)
    new_mem_k = mk_p.reshape(1, batch, MEM_LEN, MEM_HEADS, MEM_HEAD_DIM)
    new_mem_v = mv_p.reshape(1, batch, MEM_LEN, MEM_HEADS, MEM_HEAD_DIM)
    return (y_prompt, y_sample, new_k_prompt, new_v_prompt, new_k_sample, new_v_sample,
            new_mem_k, new_mem_v, d_p[None], d_s[None], c_p[None], c_s[None])
```

```python
import functools
import math

import jax
import jax.numpy as jnp
import numpy as np
from jax import lax
from jax.experimental import pallas as pl
from jax.experimental.pallas import tpu as pltpu

F32 = jnp.float32
BF16 = jnp.bfloat16

LANES = 128
D_MODEL = 2048
PAGE_SIZE = 128
DA_HEAD_DIM = 128
DA_HEADS = 8
DA_HEAD_W = 2 * DA_HEAD_DIM
GDN_HEADS = 8
GDN_DK = 128
GDN_DV = 256
GDN_QK = GDN_HEADS * GDN_DK
GDN_V = GDN_HEADS * GDN_DV
GDN_CONV_DIM = 2 * GDN_QK + GDN_V
CONV_WIDTH = 4
GDN_CHUNK = 64
GDN_GROUP = 4
GDN_SEQS_PER_STEP = 2
MEM_LEN = 256
MEM_HEADS = 4
MEM_HEAD_DIM = 128
MEM_W = MEM_HEADS * MEM_HEAD_DIM
NORM_EPS = 1e-6
SUBLN_EPS = 1e-5
LAMBDA_INIT = 0.8 - 0.6 * math.exp(-0.3 * 0)
NEW_POSITIONS = LANES // DA_HEADS
SAMPLE_ROWS = 8

R_QKV = 0
R_ZA = 4096
R_ZB = 6144
R_GA = 8192
R_GB = 10240
R_GM = 12288
R_QM = 14336
R_ZM = 14848
R_WIDTH = 15360

VMEM_LIMIT = 56 * 1024 * 1024


def _cparams(sem):
    return pltpu.CompilerParams(dimension_semantics=sem, vmem_limit_bytes=VMEM_LIMIT)


def _sigmoid(x):
    return 0.5 * jnp.tanh(0.5 * x) + 0.5


def _silu(x):
    return x * _sigmoid(x)


def _dot(a, b):
    return jnp.dot(a.astype(BF16), b.astype(BF16), preferred_element_type=F32)


def _dot_nt(a, b):
    return lax.dot_general(a.astype(BF16), b.astype(BF16), (((1,), (1,)), ((), ())),
                           preferred_element_type=F32)


def _rmsnorm_kernel(x_ref, g_ref, o_ref, *, eps):
    x = x_ref[...].astype(F32)
    y = x * lax.rsqrt(jnp.mean(x * x, axis=-1, keepdims=True) + eps)
    o_ref[...] = (y * g_ref[...]).astype(o_ref.dtype)


def _rmsnorm(x2d, g, out_dtype, tm):
    m, d = x2d.shape
    tm = min(tm, m)
    return pl.pallas_call(
        functools.partial(_rmsnorm_kernel, eps=NORM_EPS),
        grid=(m // tm,),
        in_specs=[pl.BlockSpec((tm, d), lambda i: (i, 0)),
                  pl.BlockSpec((1, d), lambda i: (0, 0))],
        out_specs=pl.BlockSpec((tm, d), lambda i: (i, 0)),
        out_shape=jax.ShapeDtypeStruct((m, d), out_dtype),
        compiler_params=_cparams(("parallel",)),
        name="rmsnorm",
    )(x2d, g.reshape(1, d).astype(F32))


def _matmul_kernel(h_ref, w_ref, *o_refs, scale):
    acc = jnp.dot(h_ref[...], w_ref[...], preferred_element_type=F32)
    if scale is not None:
        acc = acc * scale
    for o_ref in o_refs:
        o_ref[...] = acc.astype(o_ref.dtype)


def _matmul(h, w, out_dtypes, tm, tn, scale=None, name="matmul"):
    m, k = h.shape
    n = w.shape[1]
    tm, tn = min(tm, m), min(tn, n)
    outs = pl.pallas_call(
        functools.partial(_matmul_kernel, scale=scale),
        grid=(m // tm, n // tn),
        in_specs=[pl.BlockSpec((tm, k), lambda i, j: (i, 0)),
                  pl.BlockSpec((k, tn), lambda i, j: (0, j))],
        out_specs=[pl.BlockSpec((tm, tn), lambda i, j: (i, j)) for _ in out_dtypes],
        out_shape=[jax.ShapeDtypeStruct((m, n), dt) for dt in out_dtypes],
        compiler_params=_cparams(("parallel", "parallel")),
        name=name,
    )(h, w)
    return outs


def _proj_k_kernel(h_ref, w_ref, k_ref, kb_ref, ba_ref):
    acc = jnp.dot(h_ref[...], w_ref[...], preferred_element_type=F32)
    tm, n = kb_ref.shape
    parts = n // DA_HEAD_DIM
    for j in range(parts):
        k_ref[pl.ds(j, tm, stride=parts), :] = acc[:, j * DA_HEAD_DIM:(j + 1) * DA_HEAD_DIM]
    kb_ref[...] = acc[:, :n].astype(kb_ref.dtype)
    ba_ref[...] = acc[:, n:]


def _proj_k(h, w_kba, tm):
    m, k = h.shape
    n = w_kba.shape[1] - LANES
    parts = n // DA_HEAD_DIM
    tm = min(tm, m)
    return pl.pallas_call(
        _proj_k_kernel,
        grid=(m // tm,),
        in_specs=[pl.BlockSpec((tm, k), lambda i: (i, 0)),
                  pl.BlockSpec(w_kba.shape, lambda i: (0, 0), pipeline_mode=pl.Buffered(1))],
        out_specs=[pl.BlockSpec((tm * parts, DA_HEAD_DIM), lambda i: (i, 0)),
                   pl.BlockSpec((tm, n), lambda i: (i, 0)),
                   pl.BlockSpec((tm, LANES), lambda i: (i, 0))],
        out_shape=[jax.ShapeDtypeStruct((m * parts, DA_HEAD_DIM), F32), jax.ShapeDtypeStruct((m, n), BF16),
                   jax.ShapeDtypeStruct((m, LANES), F32)],
        compiler_params=_cparams(("parallel",)),
        name="proj_k",
    )(h, w_kba)


def _diff_lambda(lq1_ref, lk1_ref, lq2_ref, lk2_ref):
    s1 = jnp.sum(lq1_ref[...] * lk1_ref[...], axis=-1, keepdims=True)
    s2 = jnp.sum(lq2_ref[...] * lk2_ref[...], axis=-1, keepdims=True)
    return jnp.exp(s1) - jnp.exp(s2) + LAMBDA_INIT


def _subln(od, g):
    y = od * lax.rsqrt(jnp.mean(od * od, axis=-1, keepdims=True) + SUBLN_EPS)
    return (y * g) * (1.0 - LAMBDA_INIT)


def _attn_prompt_kernel(qi_tab, ki_tab, q_ref, k_ref, v_ref, g_ref,
                        lq1_ref, lk1_ref, lq2_ref, lk2_ref, o_ref, m_sc, l_sc, acc_sc):
    step = pl.program_id(2)
    qi = qi_tab[step]
    ki = ki_tab[step]
    tq = q_ref.shape[0]
    lane_tiles = tq // LANES

    @pl.when(ki == 0)
    def _():
        m_sc[...] = jnp.full(m_sc.shape, -jnp.inf, F32)
        l_sc[...] = jnp.zeros(l_sc.shape, F32)
        acc_sc[...] = jnp.zeros(acc_sc.shape, F32)

    heads = q_ref.shape[1] // DA_HEAD_W
    reps = DA_HEAD_W // LANES

    def update(masked):
        if masked:
            row = lax.broadcasted_iota(jnp.int32, (tq, tq), 0)
            col = lax.broadcasted_iota(jnp.int32, (tq, tq), 1)
            keep = col <= row
        for hh in range(heads):
            v = v_ref[:, hh * DA_HEAD_W:(hh + 1) * DA_HEAD_W]
            for c in range(2):
                lo = hh * DA_HEAD_W + c * DA_HEAD_DIM
                s = lax.dot_general(q_ref[:, lo:lo + DA_HEAD_DIM], k_ref[:, lo:lo + DA_HEAD_DIM],
                                    (((1,), (1,)), ((), ())), preferred_element_type=F32)
                if masked:
                    s = jnp.where(keep, s, -jnp.inf)
                n = 2 * hh + c
                m_prev = m_sc[n]
                m_new = jnp.maximum(m_prev, jnp.max(s, axis=-1, keepdims=True))
                alpha = jnp.exp2(m_prev - m_new)
                p = jnp.exp2(s - pltpu.repeat(m_new, lane_tiles, axis=1))
                l_sc[n] = alpha * l_sc[n] + jnp.sum(p, axis=-1, keepdims=True)
                acc_sc[n] = (pltpu.repeat(alpha, reps, axis=1) * acc_sc[n]
                             + jnp.dot(p.astype(BF16), v, preferred_element_type=F32))
                m_sc[n] = m_new

    @pl.when(ki < qi)
    def _():
        update(False)

    @pl.when(ki == qi)
    def _():
        update(True)
        lam = _diff_lambda(lq1_ref, lk1_ref, lq2_ref, lk2_ref)
        for hh in range(heads):
            od = (acc_sc[2 * hh] / pltpu.repeat(l_sc[2 * hh], reps, axis=1)
                  - lam * (acc_sc[2 * hh + 1] / pltpu.repeat(l_sc[2 * hh + 1], reps, axis=1)))
            o_ref[:, hh * DA_HEAD_W:(hh + 1) * DA_HEAD_W] = _subln(od, g_ref[...]).astype(o_ref.dtype)


def _attn_prompt(q, k, v, subln_g, lam_vecs, batch, seq, tq, heads_per_step):
    m = batch * seq
    nq = seq // tq
    hw = heads_per_step * DA_HEAD_W
    qi_tab = np.concatenate([np.full(i + 1, i, np.int32) for i in range(nq)])
    ki_tab = np.concatenate([np.arange(i + 1, dtype=np.int32) for i in range(nq)])
    n_steps = int(qi_tab.shape[0])
    vec = pl.BlockSpec((1, DA_HEAD_DIM), lambda b, h, s, qt, kt: (0, 0))
    grid_spec = pltpu.PrefetchScalarGridSpec(
        num_scalar_prefetch=2,
        grid=(batch, DA_HEADS // heads_per_step, n_steps),
        in_specs=[
            pl.BlockSpec((tq, hw), lambda b, h, s, qt, kt: (b * nq + qt[s], h)),
            pl.BlockSpec((tq, hw), lambda b, h, s, qt, kt: (b * nq + kt[s], h)),
            pl.BlockSpec((tq, hw), lambda b, h, s, qt, kt: (b * nq + kt[s], h)),
            pl.BlockSpec((1, DA_HEAD_W), lambda b, h, s, qt, kt: (0, 0)),
            vec, vec, vec, vec,
        ],
        out_specs=pl.BlockSpec((tq, hw), lambda b, h, s, qt, kt: (b * nq + qt[s], h)),
        scratch_shapes=[pltpu.VMEM((2 * heads_per_step, tq, LANES), F32),
                        pltpu.VMEM((2 * heads_per_step, tq, LANES), F32),
                        pltpu.VMEM((2 * heads_per_step, tq, DA_HEAD_W), F32)],
    )
    return pl.pallas_call(
        _attn_prompt_kernel,
        grid_spec=grid_spec,
        out_shape=jax.ShapeDtypeStruct((m, DA_HEADS * DA_HEAD_W), BF16),
        compiler_params=_cparams(("parallel", "parallel", "arbitrary")),
        name="attn_prompt",
    )(jnp.asarray(qi_tab), jnp.asarray(ki_tab), q, k, v, subln_g, *lam_vecs)


def _attn_paged_kernel(pt_ref, q_ref, kn_ref, vn_ref, g_ref, lq1_ref, lk1_ref, lq2_ref, lk2_ref,
                       *refs, pages_per_step):
    k_refs = refs[:pages_per_step]
    v_refs = refs[pages_per_step:2 * pages_per_step]
    o_ref = refs[2 * pages_per_step]
    m_sc, l_sc, acc_sc = refs[2 * pages_per_step + 1:]
    j = pl.program_id(1)
    nq = q_ref.shape[2]
    head_bits = DA_HEADS.bit_length() - 1

    def same_head(ncols):
        row = lax.broadcasted_iota(jnp.int32, (nq, ncols), 0)
        col = lax.broadcasted_iota(jnp.int32, (nq, ncols), 1)
        return row, col, (row & (DA_HEADS - 1)) == (col & (DA_HEADS - 1))

    def update(kv_refs, keep):
        ps, alphas = [], []
        for c in range(2):
            ss = []
            for k_ref, v_ref in kv_refs:
                k_c = k_ref[0, pl.ds(c, v_ref.shape[1], stride=2), :].astype(BF16)
                s = lax.dot_general(q_ref[0, c], k_c, (((1,), (1,)), ((), ())), preferred_element_type=F32)
                ss.append(jnp.where(keep, s, -jnp.inf))
            rows = pl.ds(c * nq, nq)
            m_prev = m_sc[rows, :]
            m_new = m_prev
            for s in ss:
                m_new = jnp.maximum(m_new, jnp.max(s, axis=-1, keepdims=True))
            alpha = jnp.exp2(m_prev - m_new)
            p_c = [jnp.exp2(s - m_new) for s in ss]
            l_new = alpha * l_sc[rows, :]
            for p in p_c:
                l_new = l_new + jnp.sum(p, axis=-1, keepdims=True)
            l_sc[rows, :] = l_new
            m_sc[rows, :] = m_new
            ps.append([p.astype(BF16) for p in p_c])
            alphas.append(alpha)
        acc = jnp.concatenate(alphas, axis=0) * acc_sc[...]
        for r, (_, v_ref) in enumerate(kv_refs):
            acc = acc + jnp.dot(jnp.concatenate([ps[0][r], ps[1][r]], axis=0), v_ref[0].astype(BF16),
                                preferred_element_type=F32)
        acc_sc[...] = acc

    @pl.when(j == 0)
    def _():
        m_sc[...] = jnp.full(m_sc.shape, -jnp.inf, F32)
        l_sc[...] = jnp.zeros(l_sc.shape, F32)
        acc_sc[...] = jnp.zeros(acc_sc.shape, F32)
        row, col, keep = same_head(vn_ref.shape[1])
        keep = keep & ((col >> head_bits) <= (row >> head_bits))
        update([(kn_ref, vn_ref)], keep)

    _, _, keep_page = same_head(PAGE_SIZE * DA_HEADS)
    update(list(zip(k_refs, v_refs)), keep_page)

    @pl.when(j == pl.num_programs(1) - 1)
    def _():
        lam = _diff_lambda(lq1_ref, lk1_ref, lq2_ref, lk2_ref)
        o = acc_sc[...] / l_sc[...]
        od = o[0:nq] - lam * o[nq:2 * nq]
        o_ref[0] = _subln(od, g_ref[...])


def _attn_paged(q2, k_new, v_new, subln_g, lam_vecs, cache_k, cache_v, page_table, pages_per_step):
    bd, n_pages = page_table.shape
    n_steps = n_pages // pages_per_step
    nq = q2.shape[2]
    pt_flat = page_table.reshape(-1)
    page = lambda b, j, pt, r: (pt[b * n_pages + j * pages_per_step + r], 0, 0)
    k_specs = [pl.BlockSpec((1,) + cache_k.shape[1:], functools.partial(page, r=r)) for r in range(pages_per_step)]
    v_specs = [pl.BlockSpec((1,) + cache_v.shape[1:], functools.partial(page, r=r)) for r in range(pages_per_step)]
    vec = pl.BlockSpec((1, DA_HEAD_DIM), lambda b, j, pt: (0, 0))
    per_seq = lambda a: pl.BlockSpec((1,) + a.shape[1:], lambda b, j, pt: (b,) + (0,) * (a.ndim - 1))
    grid_spec = pltpu.PrefetchScalarGridSpec(
        num_scalar_prefetch=1,
        grid=(bd, n_steps),
        in_specs=[per_seq(q2), per_seq(k_new), per_seq(v_new),
                  pl.BlockSpec((1, DA_HEAD_W), lambda b, j, pt: (0, 0)),
                  vec, vec, vec, vec] + k_specs + v_specs,
        out_specs=pl.BlockSpec((1, nq, DA_HEAD_W), lambda b, j, pt: (b, 0, 0)),
        scratch_shapes=[pltpu.VMEM((2 * nq, 1), F32), pltpu.VMEM((2 * nq, 1), F32),
                        pltpu.VMEM((2 * nq, DA_HEAD_W), F32)],
    )
    return pl.pallas_call(
        functools.partial(_attn_paged_kernel, pages_per_step=pages_per_step),
        grid_spec=grid_spec,
        out_shape=jax.ShapeDtypeStruct((bd, nq, DA_HEAD_W), F32),
        compiler_params=_cparams(("parallel", "arbitrary")),
        name="attn_paged",
    )(pt_flat, q2, k_new, v_new, subln_g, *lam_vecs,
      *([cache_k] * pages_per_step), *([cache_v] * pages_per_step))


def _mem_attn_kernel(q_ref, z_ref, mk_ref, mv_ref, o_ref):
    scale = MEM_HEAD_DIM ** -0.5
    for h in range(MEM_HEADS):
        lo, hi = h * MEM_HEAD_DIM, (h + 1) * MEM_HEAD_DIM
        s = _dot_nt(q_ref[0, :, lo:hi], mk_ref[0, :, lo:hi]) * scale
        s = s - jnp.max(s, axis=-1, keepdims=True)
        e = jnp.exp(s)
        p = e / jnp.sum(e, axis=-1, keepdims=True)
        o = _dot(p, mv_ref[0, :, lo:hi])
        o_ref[0, :, lo:hi] = (o * _silu(z_ref[0, :, lo:hi].astype(F32))).astype(o_ref.dtype)


def _mem_attn(rest3, mk, mv, tm):
    b, length, _ = rest3.shape
    tm = min(tm, length)
    return pl.pallas_call(
        _mem_attn_kernel,
        grid=(b, length // tm),
        in_specs=[pl.BlockSpec((1, tm, MEM_W), lambda i, j: (i, j, R_QM // MEM_W)),
                  pl.BlockSpec((1, tm, MEM_W), lambda i, j: (i, j, R_ZM // MEM_W)),
                  pl.BlockSpec((1, MEM_LEN, MEM_W), lambda i, j: (i, 0, 0)),
                  pl.BlockSpec((1, MEM_LEN, MEM_W), lambda i, j: (i, 0, 0))],
        out_specs=pl.BlockSpec((1, tm, MEM_W), lambda i, j: (i, j, 0)),
        out_shape=jax.ShapeDtypeStruct((b, length, MEM_W), BF16),
        compiler_params=_cparams(("parallel", "parallel")),
        name="mem_attn",
    )(rest3, rest3, mk, mv)


def _softplus(x):
    return jnp.maximum(x, 0.0) + jnp.log1p(jnp.exp(-jnp.abs(x)))


def _l2n(x):
    return x * lax.rsqrt(jnp.sum(x * x, axis=-1, keepdims=True) + 1e-6)


def _gdn_kernel(qkv_ref, zb_ref, ba_ref, conv0_ref, s0_ref, convw_ref, alog_ref, dtb_ref, gng_ref,
                yb_ref, sout_ref, convout_ref, xp_sc, s_sc, *, valid_len):
    for bb in range(qkv_ref.shape[0]):
        one = lambda ref: ref.at[pl.ds(bb, 1)]
        _gdn_sequence(one(qkv_ref), one(zb_ref), one(ba_ref), one(conv0_ref), one(s0_ref),
                      convw_ref, alog_ref, dtb_ref, gng_ref, one(yb_ref), one(sout_ref), one(convout_ref),
                      xp_sc.at[bb], s_sc.at[bb], valid_len=valid_len)


def _gdn_sequence(qkv_ref, zb_ref, ba_ref, conv0_ref, s0_ref, convw_ref, alog_ref, dtb_ref, gng_ref,
                  yb_ref, sout_ref, convout_ref, xp_sc, s_sc, *, valid_len):
    i = pl.program_id(1)
    c = GDN_CHUNK
    gw = GDN_GROUP * c
    tail = CONV_WIDTH - 1

    @pl.when(i == 0)
    def _():
        xp_sc[8 - tail:8, :] = conv0_ref[0]
        s_sc[...] = s0_ref[0]

    xp_sc[8:8 + c, :] = qkv_ref[0].astype(F32)

    def conv_silu(lo, width):
        y = xp_sc[8 - tail:8 - tail + c, lo:lo + width] * convw_ref[0:1, lo:lo + width]
        for t in range(1, CONV_WIDTH):
            y = y + xp_sc[8 - tail + t:8 - tail + t + c, lo:lo + width] * convw_ref[t:t + 1, lo:lo + width]
        return _silu(y)

    @pl.when(i == pl.num_programs(1) - 1)
    def _():
        convout_ref[0] = xp_sc[8 + valid_len - tail:8 + valid_len, :]

    ba = ba_ref[0]
    if valid_len < c:
        live = lax.broadcasted_iota(jnp.int32, (c, 1), 0) < valid_len

    ii = lax.broadcasted_iota(jnp.int32, (gw, gw), 0)
    jj = lax.broadcasted_iota(jnp.int32, (gw, gw), 1)
    shift = int(math.log2(c))
    same = lax.shift_right_logical(ii, shift) == lax.shift_right_logical(jj, shift)
    incl = same & (jj <= ii)
    strict = same & (jj < ii)
    upper = same & (ii <= jj)
    eye = ii == jj

    n_squarings = max(math.ceil(math.log2(valid_len)) - 1, 0)

    for grp in range(GDN_HEADS // GDN_GROUP):
        heads = [grp * GDN_GROUP + t for t in range(GDN_GROUP)]
        q_g = conv_silu(grp * GDN_GROUP * GDN_DK, GDN_GROUP * GDN_DK)
        k_g = conv_silu(GDN_QK + grp * GDN_GROUP * GDN_DK, GDN_GROUP * GDN_DK)
        v_g = conv_silu(2 * GDN_QK + grp * GDN_GROUP * GDN_DV, GDN_GROUP * GDN_DV)
        qs, ks, vs, bcols, gcols = [], [], [], [], []
        for t, h in enumerate(heads):
            qs.append(_l2n(q_g[:, t * GDN_DK:(t + 1) * GDN_DK]) * (GDN_DK ** -0.5))
            ks.append(_l2n(k_g[:, t * GDN_DK:(t + 1) * GDN_DK]))
            vs.append(v_g[:, t * GDN_DV:(t + 1) * GDN_DV])
            beta = _sigmoid(ba[:, h:h + 1])
            g = -jnp.exp(alog_ref[0:1, h:h + 1]) * _softplus(
                ba[:, GDN_HEADS + h:GDN_HEADS + h + 1] + dtb_ref[0:1, h:h + 1])
            if valid_len < c:
                beta = jnp.where(live, beta, 0.0)
                g = jnp.where(live, g, 0.0)
            bcols.append(beta)
            gcols.append(g)
        q_st = jnp.concatenate(qs, axis=0)
        k_st = jnp.concatenate(ks, axis=0)
        v_st = jnp.concatenate(vs, axis=0)
        bcol = jnp.concatenate(bcols, axis=0)
        gcol = jnp.concatenate(gcols, axis=0)

        grow = jnp.sum(jnp.where(eye, gcol, 0.0), axis=0, keepdims=True)
        gc_col = jnp.sum(jnp.where(incl, grow, 0.0), axis=1, keepdims=True)
        gc_row = jnp.sum(jnp.where(upper, gcol, 0.0), axis=0, keepdims=True)
        dmat = jnp.exp(jnp.where(incl, gc_col - gc_row, -jnp.inf))
        egc = jnp.exp(gc_col)

        k_b = k_st.astype(BF16)
        kk = _dot_nt(k_b, k_b)
        pw = -jnp.where(strict, bcol * kk * dmat, 0.0)
        pw_b = pw.astype(BF16)
        yinv = pw
        for _ in range(n_squarings):
            pw = _dot(pw_b, pw_b)
            pw_b = pw.astype(BF16)
            yinv = yinv + pw + _dot(yinv, pw_b)
        yinv_b = yinv.astype(BF16)

        rhs_v = bcol * v_st
        rhs_k = (bcol * egc) * k_st
        uv = rhs_v + _dot(yinv_b, rhs_v)
        wk = rhs_k + _dot(yinv_b, rhs_k)
        aqk = _dot_nt(q_st, k_b) * dmat

        us, qss, s_olds = [], [], []
        for t, h in enumerate(heads):
            r0, r1 = t * c, (t + 1) * c
            s_old = s_sc[h]
            res = _dot(jnp.concatenate([wk[r0:r1], q_st[r0:r1]], axis=0), s_old)
            us.append(uv[r0:r1] - res[0:c])
            qss.append(res[c:2 * c])
            s_olds.append(s_old)
        au = _dot(aqk, jnp.concatenate(us, axis=0))
        for t, h in enumerate(heads):
            r0, r1 = t * c, (t + 1) * c
            o = egc[r0:r1] * qss[t] + au[r0:r1]
            gc_last = gc_col[r1 - 1:r1]
            kdec = k_st[r0:r1] * jnp.exp(gc_last - gc_col[r0:r1])
            s_sc[h] = jnp.exp(gc_last) * s_olds[t] + _dot(kdec.T, us[t])
            on = o * lax.rsqrt(jnp.mean(o * o, axis=-1, keepdims=True) + NORM_EPS) * gng_ref[...]
            z = zb_ref[0, :, h * GDN_DV:(h + 1) * GDN_DV].astype(F32)
            yb_ref[0, :, h * GDN_DV:(h + 1) * GDN_DV] = (on * _silu(z)).astype(yb_ref.dtype)

    xp_sc[8 - tail:8, :] = xp_sc[8 + c - tail:8 + c, :]

    @pl.when(i == pl.num_programs(1) - 1)
    def _():
        sout_ref[0] = s_sc[...]


def _gdn(qkv3, qkv_blk, zb3, zb_blk, ba3, conv0, s0, conv_w, a_log, dt_bias, gng, valid_len):
    b, length = ba3.shape[0], ba3.shape[1]
    c = GDN_CHUNK
    n = length // c
    tail = CONV_WIDTH - 1
    nb = GDN_SEQS_PER_STEP if b % GDN_SEQS_PER_STEP == 0 else 1
    const = lambda *shape: pl.BlockSpec(shape, lambda bi, i: (0,) * len(shape))
    return pl.pallas_call(
        functools.partial(_gdn_kernel, valid_len=valid_len),
        grid=(b // nb, n),
        in_specs=[
            pl.BlockSpec((nb, c, GDN_CONV_DIM), lambda bi, i: (bi, i, qkv_blk)),
            pl.BlockSpec((nb, c, GDN_V), lambda bi, i: (bi, i, zb_blk)),
            pl.BlockSpec((nb, c, 128), lambda bi, i: (bi, i, 0)),
            pl.BlockSpec((nb, tail, GDN_CONV_DIM), lambda bi, i: (bi, 0, 0)),
            pl.BlockSpec((nb, GDN_HEADS, GDN_DK, GDN_DV), lambda bi, i: (bi, 0, 0, 0)),
            const(CONV_WIDTH, GDN_CONV_DIM), const(1, GDN_HEADS), const(1, GDN_HEADS), const(1, GDN_DV),
        ],
        out_specs=[
            pl.BlockSpec((nb, c, GDN_V), lambda bi, i: (bi, i, 0)),
            pl.BlockSpec((nb, GDN_HEADS, GDN_DK, GDN_DV), lambda bi, i: (bi, 0, 0, 0)),
            pl.BlockSpec((nb, tail, GDN_CONV_DIM), lambda bi, i: (bi, 0, 0)),
        ],
        out_shape=[
            jax.ShapeDtypeStruct((b, length, GDN_V), BF16),
            jax.ShapeDtypeStruct((b, GDN_HEADS, GDN_DK, GDN_DV), F32),
            jax.ShapeDtypeStruct((b, tail, GDN_CONV_DIM), F32),
        ],
        scratch_shapes=[pltpu.VMEM((nb, 8 + c, GDN_CONV_DIM), F32),
                        pltpu.VMEM((nb, GDN_HEADS, GDN_DK, GDN_DV), F32)],
        compiler_params=_cparams(("parallel", "arbitrary")),
        name="gdn",
    )(qkv3, zb3, ba3, conv0, s0, conv_w.astype(F32), a_log.reshape(1, -1).astype(F32),
      dt_bias.reshape(1, -1).astype(F32), gng.reshape(1, -1).astype(F32))


def _out_kernel(x_ref, oa_ref, za_ref, yb_ref, ym_ref, ga_ref, gb_ref, gm_ref,
                wda_ref, wdb_ref, wdm_ref, wout_ref, fg_ref, o_ref):
    sig = lambda r: _sigmoid(r[...].astype(F32))
    ya = oa_ref[...].astype(F32) * _silu(za_ref[...].astype(F32))
    merged = sig(ga_ref) * _dot(ya, wda_ref[...])
    merged = merged + sig(gb_ref) * _dot(yb_ref[...], wdb_ref[...])
    merged = merged + sig(gm_ref) * _dot(ym_ref[...], wdm_ref[...])
    r = x_ref[...] + _dot(merged, wout_ref[...])
    y = r * lax.rsqrt(jnp.mean(r * r, axis=-1, keepdims=True) + NORM_EPS)
    o_ref[...] = y * fg_ref[...]


def _out_proj(x2d, oa, yb, ym, rest, wda, wdb, wdm, wout, final_g, tm):
    m, d = x2d.shape
    tm = min(tm, m)
    row = lambda width, blk: pl.BlockSpec((tm, width), lambda i: (i, blk))
    resident = lambda shape: pl.BlockSpec(shape, lambda i: (0, 0), pipeline_mode=pl.Buffered(1))
    return pl.pallas_call(
        _out_kernel,
        grid=(m // tm,),
        in_specs=[row(d, 0), row(d, 0), row(d, R_ZA // d), row(d, 0), row(MEM_W, 0),
                  row(d, R_GA // d), row(d, R_GB // d), row(d, R_GM // d),
                  resident(wda.shape), resident(wdb.shape), resident(wdm.shape), resident(wout.shape),
                  resident((1, d))],
        out_specs=row(d, 0),
        out_shape=jax.ShapeDtypeStruct((m, d), F32),
        compiler_params=_cparams(("parallel",)),
        name="out_proj",
    )(x2d, oa, rest, yb, ym, rest, rest, rest, wda, wdb, wdm, wout, final_g.reshape(1, d).astype(F32))


def _project(h, w_q, w_kba, w_v, w_rest, tm):
    scale = DA_HEAD_DIM ** -0.5 * math.log2(math.e)
    (q,) = _matmul(h, w_q, [BF16], tm, 1024, scale=scale, name="proj_q")
    k_f32, k, ba = _proj_k(h, w_kba, min(tm, 512))
    v_f32, v = _matmul(h, w_v, [F32, BF16], tm, 1024, name="proj_v")
    (rest,) = _matmul(h, w_rest, [BF16], tm, 1024, name="proj_rest")
    return q, k_f32, k, v_f32, v, rest, ba


def kernel(x_prompt, x_sample, cache_k, cache_v, cache_mem_k, cache_mem_v, state_delta, state_conv, page_table, mem_prompt, norm_g, w_in, lambda_q1, lambda_k1, lambda_q2, lambda_k2, subln_g, conv_w, a_log, dt_bias, gdn_norm_g, w_down_a, w_down_b, w_down_m, w_out, mem_norm_g, w_mem_kv, final_norm_g):
    assert w_in.shape[0] == 1, "single-layer trunk"
    batch, seq, d = x_prompt.shape
    bd, n_new, _ = x_sample.shape
    assert d == D_MODEL and 2 * n_new == SAMPLE_ROWS and n_new >= CONV_WIDTH - 1
    assert seq % 512 == 0 and seq % GDN_CHUNK == 0

    w = w_in[0]
    o_qa, o_ka, o_za, o_qb, o_zb, o_b, o_qm, o_ga = 0, 2048, 6144, 8192, 12288, 14336, 14352, 15376
    o_va = o_ka + DA_HEADS * DA_HEAD_W
    w_q = w[:, o_qa:o_ka].astype(BF16)
    w_kba = jnp.concatenate([w[:, o_ka:o_va], w[:, o_b:o_qm],
                             jnp.zeros((d, LANES - 2 * GDN_HEADS), w.dtype)], axis=1).astype(BF16)
    w_v = w[:, o_va:o_za].astype(BF16)
    w_rest = jnp.concatenate([w[:, o_qb:o_zb], w[:, o_za:o_qb], w[:, o_zb:o_b], w[:, o_ga:], w[:, o_qm:o_ga]],
                             axis=1).astype(BF16)
    wda, wdb, wdm, wout = (t[0].astype(BF16) for t in (w_down_a, w_down_b, w_down_m, w_out))
    lam_vecs = [t[0].reshape(1, DA_HEAD_DIM).astype(F32) for t in (lambda_q1, lambda_k1, lambda_q2, lambda_k2)]
    sg = subln_g[0].reshape(1, DA_HEAD_W).astype(F32)

    xp2 = x_prompt.reshape(batch * seq, d)
    hp = _rmsnorm(xp2, norm_g[0], BF16, 512)
    q_p, k_p_f32, k_p, v_p_f32, v_p, rest_p, ba_p = _project(hp, w_q, w_kba, w_v, w_rest, 1024)
    oa_p = _attn_prompt(q_p, k_p, v_p, sg, lam_vecs, batch, seq, 512, 4)

    hm = _rmsnorm(mem_prompt.reshape(batch * MEM_LEN, d), mem_norm_g[0], BF16, 512)
    (mem_kv,) = _matmul(hm, w_mem_kv[0].astype(BF16), [F32], 512, 1024, name="proj_mem_kv")
    mk_p = mem_kv[:, :MEM_W].reshape(batch, MEM_LEN, MEM_W)
    mv_p = mem_kv[:, MEM_W:].reshape(batch, MEM_LEN, MEM_W)
    rest_p3 = rest_p.reshape(batch, seq, R_WIDTH)
    ym_p = _mem_attn(rest_p3, mk_p, mv_p, 512)

    conv0 = jnp.zeros((batch, CONV_WIDTH - 1, GDN_CONV_DIM), F32)
    s0 = jnp.zeros((batch, GDN_HEADS, GDN_DK, GDN_DV), F32)
    yb_p, d_p, c_p = _gdn(rest_p3, 0, rest_p3, R_ZB // GDN_V, ba_p.reshape(batch, seq, 128), conv0, s0,
                          conv_w[0], a_log[0], dt_bias[0], gdn_norm_g[0], GDN_CHUNK)

    y_p = _out_proj(xp2, oa_p, yb_p.reshape(batch * seq, GDN_V), ym_p.reshape(batch * seq, MEM_W), rest_p,
                    wda, wdb, wdm, wout, final_norm_g, 256)

    rows = SAMPLE_ROWS
    xs2 = jnp.pad(x_sample, ((0, 0), (0, rows - n_new), (0, 0))).reshape(bd * rows, d)
    hs = _rmsnorm(xs2, norm_g[0], BF16, bd * rows)
    q_s, k_s_f32, _, v_s_f32, _, rest_s, ba_s = _project(hs, w_q, w_kba, w_v, w_rest, bd * rows)
    k_s3 = k_s_f32.reshape(bd, rows, DA_HEADS * DA_HEAD_W)
    v_s3 = v_s_f32.reshape(bd, rows, DA_HEADS * DA_HEAD_W)
    rest_s3 = rest_s.reshape(bd, rows, R_WIDTH)

    q2 = q_s.reshape(bd, rows, DA_HEADS, 2, DA_HEAD_DIM)[:, :n_new]
    q2 = jnp.transpose(q2, (0, 3, 1, 2, 4)).reshape(bd, 2, n_new * DA_HEADS, DA_HEAD_DIM)
    pad_new = ((0, 0), (0, NEW_POSITIONS - rows), (0, 0))
    k_new = jnp.pad(k_s3, pad_new).reshape(bd, NEW_POSITIONS * DA_HEADS * 2, DA_HEAD_DIM)
    v_new = jnp.pad(v_s3, pad_new).reshape(bd, NEW_POSITIONS * DA_HEADS, DA_HEAD_W)
    n_phys = cache_k.shape[1]
    ck = cache_k.reshape(n_phys, PAGE_SIZE * DA_HEADS * 2, DA_HEAD_DIM)
    cv = cache_v.reshape(n_phys, PAGE_SIZE * DA_HEADS, DA_HEAD_W)
    oa_s = _attn_paged(q2, k_new, v_new, sg, lam_vecs, ck, cv, page_table, 8)
    oa_s = jnp.pad(oa_s.reshape(bd, n_new, d), ((0, 0), (0, rows - n_new), (0, 0)))

    mk_s = cache_mem_k[0].reshape(bd, MEM_LEN, MEM_W)
    mv_s = cache_mem_v[0].reshape(bd, MEM_LEN, MEM_W)
    ym_s = _mem_attn(rest_s3, mk_s, mv_s, rows)

    pad_c = ((0, 0), (0, GDN_CHUNK - rows), (0, 0))
    qkv_s = jnp.pad(rest_s3[:, :, R_QKV:R_QKV + GDN_CONV_DIM], pad_c)
    zb_s = jnp.pad(rest_s3[:, :, R_ZB:R_ZB + GDN_V], pad_c)
    ba_s3 = jnp.pad(ba_s.reshape(bd, rows, 128), pad_c)
    yb_s, d_s, c_s = _gdn(qkv_s, 0, zb_s, 0, ba_s3, state_conv[0].astype(F32), state_delta[0].astype(F32),
                          conv_w[0], a_log[0], dt_bias[0], gdn_norm_g[0], n_new)
    yb_s2 = yb_s[:, :rows].reshape(bd * rows, GDN_V)

    y_s = _out_proj(xs2, oa_s.reshape(bd * rows, d), yb_s2, ym_s.reshape(bd * rows, MEM_W), rest_s,
                    wda, wdb, wdm, wout, final_norm_g, bd * rows)

    y_prompt = y_p.reshape(batch, seq, d)
    y_sample = y_s.reshape(bd, rows, d)[:, :n_new]
    new_k_prompt = k_p_f32.reshape(1, batch, seq, DA_HEADS, 2, DA_HEAD_DIM)
    new_v_prompt = v_p_f32.reshape(1, batch, seq, DA_HEADS, DA_HEAD_W)
    new_k_sample = k_s3[:, :n_new].reshape(1, bd, n_new, DA_HEADS, 2, DA_HEAD_DIM)
    new_v_sample = v_s3[:, :n_new].reshape(1, bd, n_new, DA_HEADS, DA_HEAD_W)
    new_mem_k = mk_p.reshape(1, batch, MEM_LEN, MEM_HEADS, MEM_HEAD_DIM)
    new_mem_v = mv_p.reshape(1, batch, MEM_LEN, MEM_HEADS, MEM_HEAD_DIM)
    return (y_prompt, y_sample, new_k_prompt, new_v_prompt, new_k_sample, new_v_sample,
            new_mem_k, new_mem_v, d_p[None], d_s[None], c_p[None], c_s[None])
```
